```python
import jax, jax.numpy as jnp
from jax import lax
import numpy as np

D_MODEL = 1024
BATCH = 16
SEQ = 2048
DEPTH = 4

N_EVEN = (DEPTH + 1) // 2
N_ODD = DEPTH // 2
D_FF = 2816
NORM_EPS = 1e-6

ATTN_HEADS = 8
HEAD_DIM = 64
ATTN_WIDTH = ATTN_HEADS * HEAD_DIM
ROT_DIM = HEAD_DIM // 4
ROPE_THETA = 500000.0
MOBA_BLOCK = 256
MOBA_TOPK = 3
MOBA_Q_CHUNK = 32

LRU_WIDTH = 512
LRU_HEADS = 8
LRU_HEAD_DIM = LRU_WIDTH // LRU_HEADS
LRU_CONV = 4
LRU_C = 8.0

AB_IN = 3 * ATTN_WIDTH + 2 * LRU_WIDTH
AB_OUT = ATTN_WIDTH + LRU_WIDTH

SC_WIDTH = D_MODEL
SC_CONV = 3

kernel_name = "hybrid_rglru_moba_shortconv_macaron"


def rms_norm(x, g):
    xf = x.astype(jnp.float32)
    y = xf * lax.rsqrt(jnp.mean(xf * xf, axis=-1, keepdims=True) + NORM_EPS) * g.astype(jnp.float32)
    return y.astype(x.dtype)


def swiglu(h, w_gate, w_up, w_down):
    return (jax.nn.silu(h @ w_gate) * (h @ w_up)) @ w_down


def causal_depthwise_conv(x, w):
    k, c = w.shape
    return lax.conv_general_dilated(
        x, w[:, None, :].astype(x.dtype), window_strides=(1,), padding=[(k - 1, 0)],
        dimension_numbers=("NWC", "WIO", "NWC"), feature_group_count=c)


def partial_rope(x, pos):
    half = ROT_DIM // 2
    inv_freq = ROPE_THETA ** (-jnp.arange(0, ROT_DIM, 2, dtype=jnp.float32) / ROT_DIM)
    ang = pos[:, None] * inv_freq[None, :]
    cos = jnp.cos(ang)[None, :, None, :]
    sin = jnp.sin(ang)[None, :, None, :]
    xf = x.astype(jnp.float32)
    x1, x2, rest = xf[..., :half], xf[..., half:ROT_DIM], xf[..., ROT_DIM:]
    out = jnp.concatenate([x1 * cos - x2 * sin, x2 * cos + x1 * sin, rest], axis=-1)
    return out.astype(x.dtype)


def moba_attention(q, k, v):
    b, s, h, hd = q.shape
    n_blk = -(-s // MOBA_BLOCK)
    sp = n_blk * MOBA_BLOCK
    pad = ((0, 0), (0, sp - s), (0, 0), (0, 0))
    qt = jnp.pad(q, pad).transpose(0, 2, 1, 3)
    kt = jnp.pad(k, pad).transpose(0, 2, 1, 3)
    vt = jnp.pad(v, pad).transpose(0, 2, 1, 3)
    kb = kt.reshape(b, h, n_blk, MOBA_BLOCK, hd)
    vb = vt.reshape(b, h, n_blk, MOBA_BLOCK, hd)
    kmean = jnp.mean(kb.astype(jnp.float32), axis=3)
    n_sel = min(MOBA_TOPK, n_blk - 1)
    scale = hd ** -0.5
    blk_ids = jnp.arange(n_blk)
    bi = jnp.arange(b)[:, None, None]
    hi = jnp.arange(h)[None, :, None]
    neg = jnp.float32(-1e30)

    def chunk(c):
        start = c * MOBA_Q_CHUNK
        own = start // MOBA_BLOCK
        qc = lax.dynamic_slice_in_dim(qt, start, MOBA_Q_CHUNK, axis=2)
        qpos = start + jnp.arange(MOBA_Q_CHUNK)
        k_own = lax.dynamic_index_in_dim(kb, own, axis=2, keepdims=False)
        v_own = lax.dynamic_index_in_dim(vb, own, axis=2, keepdims=False)
        kpos = own * MOBA_BLOCK + jnp.arange(MOBA_BLOCK)
        s_own = jnp.einsum("bhqd,bhkd->bhqk", qc, k_own).astype(jnp.float32) * scale
        s_own = jnp.where(kpos[None, :] <= qpos[:, None], s_own, neg)
        logits = [s_own]
        vals = [v_own]
        if n_sel > 0:
            g = jnp.einsum("bhqd,bhnd->bhqn", qc.astype(jnp.float32), kmean)
            g = jnp.where(blk_ids < own, g, neg)
            _, idx = lax.top_k(g, n_sel)
            valid = idx < own
            for j in range(n_sel):
                ks = kb[bi, hi, idx[..., j]]
                vs = vb[bi, hi, idx[..., j]]
                sj = jnp.einsum("bhqd,bhqkd->bhqk", qc, ks).astype(jnp.float32) * scale
                logits.append(jnp.where(valid[..., j][..., None], sj, neg))
                vals.append(vs)
        p = jax.nn.softmax(jnp.concatenate(logits, axis=-1), axis=-1)
        p = p.astype(v.dtype)
        out = jnp.einsum("bhqk,bhkd->bhqd", p[..., :MOBA_BLOCK], vals[0])
        for j in range(1, n_sel + 1):
            pj = p[..., j * MOBA_BLOCK:(j + 1) * MOBA_BLOCK]
            out = out + jnp.einsum("bhqk,bhqkd->bhqd", pj, vals[j])
        return out

    outs = lax.map(chunk, jnp.arange(sp // MOBA_Q_CHUNK))
    outs = outs.transpose(1, 0, 3, 2, 4).reshape(b, sp, h, hd)[:, :s]
    return outs.reshape(b, s, h * hd)


def rg_lru(xc, wa, ba, wx, bx, lam):
    b, s, w = xc.shape
    xf = xc.astype(jnp.float32).reshape(b, s, LRU_HEADS, LRU_HEAD_DIM)
    r = jax.nn.sigmoid(jnp.einsum("bshi,hij->bshj", xf, wa.astype(jnp.float32)) + ba.astype(jnp.float32))
    i = jax.nn.sigmoid(jnp.einsum("bshi,hij->bshj", xf, wx.astype(jnp.float32)) + bx.astype(jnp.float32))
    log_a = -LRU_C * r * jax.nn.softplus(-lam.astype(jnp.float32)).reshape(LRU_HEADS, LRU_HEAD_DIM)
    a = jnp.exp(log_a)
    u = jnp.sqrt(-jnp.expm1(2.0 * log_a)) * (i * xf)
    a = a.reshape(b, s, w)
    u = u.reshape(b, s, w)

    def combine(e1, e2):
        a1, u1 = e1
        a2, u2 = e2
        return a1 * a2, a2 * u1 + u2

    _, hseq = lax.associative_scan(combine, (a, u), axis=1)
    return hseq.astype(xc.dtype)


def setup_inputs(seed: int = 0) -> dict:
    key = jax.random.key(seed)
    ks = jax.random.split(key, 24)
    f32 = jnp.float32

    def nrm(k, shape, fan_in):
        return jax.random.normal(k, shape, f32) * (fan_in ** -0.5)

    x = jax.random.normal(ks[0], (BATCH, SEQ, D_MODEL), f32)
    ffn1_w_gate = nrm(ks[1], (DEPTH, D_MODEL, D_FF), D_MODEL)
    ffn1_w_up = nrm(ks[2], (DEPTH, D_MODEL, D_FF), D_MODEL)
    ffn1_w_down = nrm(ks[3], (DEPTH, D_FF, D_MODEL), D_FF)
    ffn2_w_gate = nrm(ks[4], (DEPTH, D_MODEL, D_FF), D_MODEL)
    ffn2_w_up = nrm(ks[5], (DEPTH, D_MODEL, D_FF), D_MODEL)
    ffn2_w_down = nrm(ks[6], (DEPTH, D_FF, D_MODEL), D_FF)
    norm_pre = 1.0 + 0.05 * jax.random.normal(ks[7], (DEPTH, 3, D_MODEL), f32)
    norm_post = 1.0 + 0.05 * jax.random.normal(ks[8], (DEPTH, 3, D_MODEL), f32)
    ab_w_in = nrm(ks[9], (N_EVEN, D_MODEL, AB_IN), D_MODEL)
    ab_w_out = nrm(ks[10], (N_EVEN, AB_OUT, D_MODEL), AB_OUT)
    lru_conv_w = nrm(ks[11], (N_EVEN, LRU_CONV, LRU_WIDTH), LRU_CONV)
    lru_conv_b = 0.01 * jax.random.normal(ks[12], (N_EVEN, LRU_WIDTH), f32)
    lru_gate_a_w = nrm(ks[13], (N_EVEN, LRU_HEADS, LRU_HEAD_DIM, LRU_HEAD_DIM), LRU_HEAD_DIM)
    lru_gate_a_b = 0.01 * jax.random.normal(ks[14], (N_EVEN, LRU_HEADS, LRU_HEAD_DIM), f32)
    lru_gate_x_w = nrm(ks[15], (N_EVEN, LRU_HEADS, LRU_HEAD_DIM, LRU_HEAD_DIM), LRU_HEAD_DIM)
    lru_gate_x_b = 0.01 * jax.random.normal(ks[16], (N_EVEN, LRU_HEADS, LRU_HEAD_DIM), f32)
    a_c = jax.random.uniform(ks[17], (N_EVEN, LRU_WIDTH), f32, 0.9, 0.999)
    a0 = a_c ** (1.0 / LRU_C)
    lru_lambda = jnp.log(a0) - jnp.log1p(-a0)
    c_w_in = nrm(ks[18], (N_ODD, D_MODEL, 3 * SC_WIDTH), D_MODEL)
    c_conv_w = nrm(ks[19], (N_ODD, SC_CONV, SC_WIDTH), SC_CONV)
    c_w_out = nrm(ks[20], (N_ODD, SC_WIDTH, D_MODEL), SC_WIDTH)
    return {
        "x": x,
        "ffn1_w_gate": ffn1_w_gate, "ffn1_w_up": ffn1_w_up, "ffn1_w_down": ffn1_w_down,
        "ffn2_w_gate": ffn2_w_gate, "ffn2_w_up": ffn2_w_up, "ffn2_w_down": ffn2_w_down,
        "norm_pre": norm_pre, "norm_post": norm_post,
        "ab_w_in": ab_w_in, "ab_w_out": ab_w_out,
        "lru_conv_w": lru_conv_w, "lru_conv_b": lru_conv_b,
        "lru_gate_a_w": lru_gate_a_w, "lru_gate_a_b": lru_gate_a_b,
        "lru_gate_x_w": lru_gate_x_w, "lru_gate_x_b": lru_gate_x_b,
        "lru_lambda": lru_lambda,
        "c_w_in": c_w_in, "c_conv_w": c_conv_w, "c_w_out": c_w_out,
    }


def reference(x, ffn1_w_gate, ffn1_w_up, ffn1_w_down, ffn2_w_gate, ffn2_w_up, ffn2_w_down,
              norm_pre, norm_post, ab_w_in, ab_w_out, lru_conv_w, lru_conv_b,
              lru_gate_a_w, lru_gate_a_b, lru_gate_x_w, lru_gate_x_b, lru_lambda,
              c_w_in, c_conv_w, c_w_out):
    b, s, d = x.shape
    pos = jnp.arange(s, dtype=jnp.float32)
    for l in range(DEPTH):
        hdn = swiglu(rms_norm(x, norm_pre[l, 0]), ffn1_w_gate[l], ffn1_w_up[l], ffn1_w_down[l])
        x = x + 0.5 * rms_norm(hdn, norm_post[l, 0])

        h = rms_norm(x, norm_pre[l, 1])
        if l % 2 == 0:
            e = l // 2
            p = h @ ab_w_in[e]
            q = p[..., :ATTN_WIDTH].reshape(b, s, ATTN_HEADS, HEAD_DIM)
            k = p[..., ATTN_WIDTH:2 * ATTN_WIDTH].reshape(b, s, ATTN_HEADS, HEAD_DIM)
            v = p[..., 2 * ATTN_WIDTH:3 * ATTN_WIDTH].reshape(b, s, ATTN_HEADS, HEAD_DIM)
            lru_x = p[..., 3 * ATTN_WIDTH:3 * ATTN_WIDTH + LRU_WIDTH]
            lru_g = p[..., 3 * ATTN_WIDTH + LRU_WIDTH:]
            attn = moba_attention(partial_rope(q, pos), partial_rope(k, pos), v)
            xc = causal_depthwise_conv(lru_x, lru_conv_w[e]) + lru_conv_b[e]
            rec = rg_lru(xc, lru_gate_a_w[e], lru_gate_a_b[e], lru_gate_x_w[e], lru_gate_x_b[e],
                         lru_lambda[e]) * jax.nn.gelu(lru_g)
            mix = jnp.concatenate([attn, rec], axis=-1) @ ab_w_out[e]
        else:
            o = l // 2
            p = h @ c_w_in[o]
            gate_b = p[..., :SC_WIDTH]
            gate_c = p[..., SC_WIDTH:2 * SC_WIDTH]
            xt = p[..., 2 * SC_WIDTH:]
            mix = (gate_b * causal_depthwise_conv(gate_c * xt, c_conv_w[o])) @ c_w_out[o]
        x = x + rms_norm(mix, norm_post[l, 1])

        hdn = swiglu(rms_norm(x, norm_pre[l, 2]), ffn2_w_gate[l], ffn2_w_up[l], ffn2_w_down[l])
        x = x + 0.5 * rms_norm(hdn, norm_post[l, 2])
    return x
```

```python
import functools

import jax
import jax.numpy as jnp
from jax import lax
from jax.experimental import pallas as pl
from jax.experimental.pallas import tpu as pltpu

F32 = jnp.float32
BF16 = jnp.bfloat16

NORM_EPS = 1e-6
D_MODEL = 1024
D_FF = 2816

ATTN_HEADS = 8
HEAD_DIM = 64
ATTN_WIDTH = ATTN_HEADS * HEAD_DIM
ROT_DIM = HEAD_DIM // 4
ROPE_THETA = 500000.0
MOBA_BLOCK = 256
MOBA_TOPK = 3
MASK_VALUE = -1e30

LRU_WIDTH = 512
LRU_HEADS = 8
LRU_CONV = 4
LRU_C = 8.0
AB_IN = 3 * ATTN_WIDTH + 2 * LRU_WIDTH

SC_CONV = 3

V7X_LANES = 128
V7X_SUBLANES = 8
V7X_VMEM_LIMIT_BYTES = 56 * 1024 * 1024

ROW_TILE = 512
FF_CHUNK = D_FF // 2
HEADS_PER_STEP = V7X_LANES // HEAD_DIM


def _rms(x, g):
    return x * lax.rsqrt(jnp.mean(x * x, axis=-1, keepdims=True) + NORM_EPS) * g


def _dot(a, b):
    return jnp.dot(a, b, preferred_element_type=F32)


def _resident(shape, index_map):
    return pl.BlockSpec(shape, index_map, pipeline_mode=pl.Buffered(1))


def _shift_rows(cur, tail, k):
    sh = pltpu.roll(cur, k, axis=0)
    tl = pltpu.roll(tail, k, axis=0)
    row = lax.broadcasted_iota(jnp.int32, (V7X_SUBLANES, 1), 0)
    head = jnp.where(row < k, tl, sh[:V7X_SUBLANES])
    return jnp.concatenate([head, sh[V7X_SUBLANES:]], axis=0)


def _ffn_body(x_ref, gpre_ref, gpost_ref, wg_ref, wu_ref, wd_ref, o_ref):
    x = x_ref[...]
    h = _rms(x, gpre_ref[...]).astype(BF16)
    acc = None
    for c in range(D_FF // FF_CHUNK):
        sl = slice(c * FF_CHUNK, (c + 1) * FF_CHUNK)
        g = _dot(h, wg_ref[:, sl])
        u = _dot(h, wu_ref[:, sl])
        a = (g * jax.nn.sigmoid(g) * u).astype(BF16)
        d = _dot(a, wd_ref[sl, :])
        acc = d if acc is None else acc + d
    o_ref[...] = x + 0.5 * _rms(acc, gpost_ref[...])


def _ffn(x, gpre, gpost, wg, wu, wd, layer):
    m, d = x.shape
    row_spec = pl.BlockSpec((ROW_TILE, d), lambda i: (i, 0))
    vec_spec = _resident((1, d), lambda i: (0, 0))
    return pl.pallas_call(
        _ffn_body,
        grid=(m // ROW_TILE,),
        in_specs=[
            row_spec, vec_spec, vec_spec,
            _resident((None, d, D_FF), lambda i: (layer, 0, 0)),
            _resident((None, d, D_FF), lambda i: (layer, 0, 0)),
            _resident((None, D_FF, d), lambda i: (layer, 0, 0)),
        ],
        out_specs=row_spec,
        out_shape=jax.ShapeDtypeStruct((m, d), F32),
        compiler_params=pltpu.CompilerParams(
            dimension_semantics=("parallel",), vmem_limit_bytes=V7X_VMEM_LIMIT_BYTES),
        name="ffn",
    )(x, gpre, gpost, wg, wu, wd)


def _sconv_body(x_ref, gpre_ref, gpost_ref, win_ref, cw_ref, wout_ref, o_ref, tail_ref):
    @pl.when(pl.program_id(1) == 0)
    def _():
        tail_ref[...] = jnp.zeros_like(tail_ref)

    d = x_ref.shape[-1]
    x = x_ref[...]
    h = _rms(x, gpre_ref[...]).astype(BF16)
    p = _dot(h, win_ref[...])
    gate_b = p[:, :d]
    cx = p[:, d:2 * d] * p[:, 2 * d:]
    tail = tail_ref[...]
    conv = cw_ref[SC_CONV - 1:SC_CONV, :] * cx
    for k in range(1, SC_CONV):
        conv = conv + cw_ref[SC_CONV - 1 - k:SC_CONV - k, :] * _shift_rows(cx, tail, k)
    tail_ref[...] = cx[ROW_TILE - V7X_SUBLANES:, :]
    mix = _dot((gate_b * conv).astype(BF16), wout_ref[...])
    o_ref[...] = x + _rms(mix, gpost_ref[...])


def _sconv(x, gpre, gpost, w_in, conv_w, w_out, layer, batch, seq):
    m, d = x.shape
    n_s = seq // ROW_TILE
    row_spec = pl.BlockSpec((ROW_TILE, d), lambda b, s: (b * n_s + s, 0))
    vec_spec = _resident((1, d), lambda b, s: (0, 0))
    return pl.pallas_call(
        _sconv_body,
        grid=(batch, n_s),
        in_specs=[
            row_spec, vec_spec, vec_spec,
            _resident((None, d, 3 * d), lambda b, s: (layer, 0, 0)),
            _resident((None, SC_CONV, d), lambda b, s: (layer, 0, 0)),
            _resident((None, d, d), lambda b, s: (layer, 0, 0)),
        ],
        out_specs=row_spec,
        out_shape=jax.ShapeDtypeStruct((m, d), F32),
        scratch_shapes=[pltpu.VMEM((V7X_SUBLANES, d), F32)],
        compiler_params=pltpu.CompilerParams(
            dimension_semantics=("arbitrary", "arbitrary"), vmem_limit_bytes=V7X_VMEM_LIMIT_BYTES),
        name="sconv",
    )(x, gpre, gpost, w_in, conv_w, w_out)


def _rope(z, cos, sin_up, sin_dn):
    half = ROT_DIM // 2
    outs = []
    for c in range(ATTN_WIDTH // V7X_LANES):
        zc = z[:, c * V7X_LANES:(c + 1) * V7X_LANES]
        up = pltpu.roll(zc, V7X_LANES - half, axis=1)
        dn = pltpu.roll(zc, half, axis=1)
        outs.append(zc * cos + up * sin_up + dn * sin_dn)
    return jnp.concatenate(outs, axis=1)


def _scan_rows(a, u):
    n = a.shape[0]
    row = lax.broadcasted_iota(jnp.int32, (n, 1), 0)
    d = 1
    while d < n:
        if d < V7X_SUBLANES:
            keep = row >= d
            a_s = jnp.where(keep, pltpu.roll(a, d, axis=0), 1.0)
            u_s = jnp.where(keep, pltpu.roll(u, d, axis=0), 0.0)
        else:
            a_s = jnp.concatenate([jnp.ones((d, a.shape[1]), F32), a[:n - d]], axis=0)
            u_s = jnp.concatenate([jnp.zeros((d, a.shape[1]), F32), u[:n - d]], axis=0)
        u = a * u_s + u
        a = a * a_s
        d *= 2
    return a, u


def _proj_lru_body(x_ref, gpre_ref, win_ref, cos_ref, sup_ref, sdn_ref, cw_ref, cb_ref,
                   wa_ref, wx_ref, ba_ref, bx_ref, lam_ref,
                   q_ref, k_ref, v_ref, km_ref, rec_ref, xtail_ref, hprev_ref):
    @pl.when(pl.program_id(1) == 0)
    def _():
        xtail_ref[...] = jnp.zeros_like(xtail_ref)
        hprev_ref[...] = jnp.zeros_like(hprev_ref)

    w = ATTN_WIDTH
    h = _rms(x_ref[...], gpre_ref[...]).astype(BF16)
    p = _dot(h, win_ref[...])

    cos, sup, sdn = cos_ref[...], sup_ref[...], sdn_ref[...]
    q_ref[...] = (_rope(p[:, :w], cos, sup, sdn) * (HEAD_DIM ** -0.5)).astype(BF16)
    k = _rope(p[:, w:2 * w], cos, sup, sdn)
    k_ref[...] = k.astype(BF16)
    v_ref[...] = p[:, 2 * w:3 * w].astype(BF16)
    for j in range(ROW_TILE // MOBA_BLOCK):
        km_ref[j] = jnp.mean(k[j * MOBA_BLOCK:(j + 1) * MOBA_BLOCK], axis=0, keepdims=True)

    lx = p[:, 3 * w:3 * w + LRU_WIDTH]
    lg = p[:, 3 * w + LRU_WIDTH:]
    xtail = xtail_ref[...]
    xc = cw_ref[LRU_CONV - 1:LRU_CONV, :] * lx + cb_ref[...]
    for kk in range(1, LRU_CONV):
        xc = xc + cw_ref[LRU_CONV - 1 - kk:LRU_CONV - kk, :] * _shift_rows(lx, xtail, kk)
    xtail_ref[...] = lx[ROW_TILE - V7X_SUBLANES:, :]

    xcb = xc.astype(BF16)
    r = jax.nn.sigmoid(_dot(xcb, wa_ref[...]) + ba_ref[...])
    i = jax.nn.sigmoid(_dot(xcb, wx_ref[...]) + bx_ref[...])
    log_a = (-LRU_C) * r * jax.nn.softplus(-lam_ref[...])
    a = jnp.exp(log_a)
    u = jnp.sqrt(1.0 - a * a) * (i * xc)
    a_cum, h_loc = _scan_rows(a, u)
    hs = h_loc + a_cum * hprev_ref[V7X_SUBLANES - 1:V7X_SUBLANES, :]
    hprev_ref[...] = hs[ROW_TILE - V7X_SUBLANES:, :]
    rec_ref[...] = (hs * jax.nn.gelu(lg)).astype(BF16)


def _proj_lru(x, gpre, w_in, cos, sup, sdn, conv_w, conv_b, wa, wx, ba, bx, lam, layer, batch, seq):
    m, d = x.shape
    n_s = seq // ROW_TILE
    blocks_per_tile = ROW_TILE // MOBA_BLOCK

    def rows(width):
        return pl.BlockSpec((ROW_TILE, width), lambda b, s: (b * n_s + s, 0))

    def res(shape):
        zeros = (0,) * (len(shape) - 1)
        return _resident(shape, lambda b, s: (layer,) + zeros)

    tab_spec = pl.BlockSpec((ROW_TILE, V7X_LANES), lambda b, s: (s, 0))
    return pl.pallas_call(
        _proj_lru_body,
        grid=(batch, n_s),
        in_specs=[
            rows(d), _resident((1, d), lambda b, s: (0, 0)),
            res((None, d, AB_IN)),
            tab_spec, tab_spec, tab_spec,
            res((None, LRU_CONV, LRU_WIDTH)), res((None, 1, LRU_WIDTH)),
            res((None, LRU_WIDTH, LRU_WIDTH)), res((None, LRU_WIDTH, LRU_WIDTH)),
            res((None, 1, LRU_WIDTH)), res((None, 1, LRU_WIDTH)), res((None, 1, LRU_WIDTH)),
        ],
        out_specs=[
            rows(ATTN_WIDTH), rows(ATTN_WIDTH), rows(ATTN_WIDTH),
            pl.BlockSpec((blocks_per_tile, 1, ATTN_WIDTH), lambda b, s: (b * n_s + s, 0, 0)),
            rows(LRU_WIDTH),
        ],
        out_shape=[
            jax.ShapeDtypeStruct((m, ATTN_WIDTH), BF16),
            jax.ShapeDtypeStruct((m, ATTN_WIDTH), BF16),
            jax.ShapeDtypeStruct((m, ATTN_WIDTH), BF16),
            jax.ShapeDtypeStruct((m // MOBA_BLOCK, 1, ATTN_WIDTH), F32),
            jax.ShapeDtypeStruct((m, LRU_WIDTH), BF16),
        ],
        scratch_shapes=[pltpu.VMEM((V7X_SUBLANES, LRU_WIDTH), F32),
                        pltpu.VMEM((V7X_SUBLANES, LRU_WIDTH), F32)],
        compiler_params=pltpu.CompilerParams(
            dimension_semantics=("arbitrary", "arbitrary"), vmem_limit_bytes=V7X_VMEM_LIMIT_BYTES),
        name="proj_lru",
    )(x, gpre, w_in, cos, sup, sdn, conv_w, conv_b, wa, wx, ba, bx, lam)


_NT = (((1,), (1,)), ((), ()))
_TN = (((0,), (0,)), ((), ()))


def _moba_body(q_ref, k_ref, v_ref, km_ref, o_ref, bias_ref, *, n_blk):
    qi = pl.program_id(2)
    blk = MOBA_BLOCK
    nq = HEADS_PER_STEP * blk
    q = q_ref[0]
    lane = lax.broadcasted_iota(jnp.int32, (1, V7X_LANES), 1)
    zero = jnp.zeros_like(q)
    q2 = jnp.concatenate(
        [jnp.where((lane // HEAD_DIM) == hh, q, zero) for hh in range(HEADS_PER_STEP)], axis=0)

    km = km_ref[0]
    km_hi = km.astype(BF16).astype(F32)
    km_mid = (km - km_hi).astype(BF16).astype(F32)
    km_lo = (km - km_hi - km_mid).astype(BF16).astype(F32)
    km3 = jnp.concatenate([km_hi, km_mid, km_lo], axis=0).astype(BF16)
    g3 = lax.dot_general(km3, q2, _NT, preferred_element_type=F32)
    g = g3[:n_blk] + g3[n_blk:2 * n_blk] + g3[2 * n_blk:]
    bid = lax.broadcasted_iota(jnp.int32, (n_blk, nq), 0)
    g = jnp.where(bid < qi, g, MASK_VALUE)
    bias = jnp.full((n_blk, nq), MASK_VALUE, F32)
    for j in range(n_blk):
        gj = g[j:j + 1, :]
        beats = (g > gj) | ((g == gj) & (bid < j))
        rank = jnp.sum(beats.astype(jnp.int32), axis=0, keepdims=True)
        chosen = (rank < MOBA_TOPK) & (j < qi)
        bias = jnp.where((bid == j) & chosen, 0.0, bias)
    bias_ref[...] = bias

    def scores(off):
        kb = k_ref[0, pl.ds(off, blk), :]
        return lax.dot_general(kb, q2, _NT, preferred_element_type=F32)

    def weighted_values(off, p):
        vb = v_ref[0, pl.ds(off, blk), :]
        return lax.dot_general(vb, p.astype(BF16), _TN, preferred_element_type=F32)

    own = pl.multiple_of(qi * blk, blk)
    kpos = lax.broadcasted_iota(jnp.int32, (blk, nq), 0)
    qpos = lax.broadcasted_iota(jnp.int32, (blk, nq), 1) % blk
    s = jnp.where(kpos <= qpos, scores(own), MASK_VALUE)
    m0 = jnp.max(s, axis=0, keepdims=True)
    p = jnp.exp(s - m0)
    l0 = jnp.sum(p, axis=0, keepdims=True)
    acc0 = weighted_values(own, p)

    def step(j, carry):
        m, l, acc = carry
        off = pl.multiple_of(j * blk, blk)
        s = scores(off) + bias_ref[pl.ds(j, 1), :]
        m_new = jnp.maximum(m, jnp.max(s, axis=0, keepdims=True))
        alpha = jnp.exp(m - m_new)
        p = jnp.exp(s - m_new)
        l = alpha * l + jnp.sum(p, axis=0, keepdims=True)
        acc = alpha * acc + weighted_values(off, p)
        return m_new, l, acc

    _, l, acc = lax.fori_loop(0, qi, step, (m0, l0, acc0))
    out_t = acc / l
    out_t = jnp.concatenate(
        [out_t[hh * HEAD_DIM:(hh + 1) * HEAD_DIM, hh * blk:(hh + 1) * blk] for hh in range(HEADS_PER_STEP)],
        axis=0)
    o_ref[0] = out_t.T.astype(BF16)


def _moba(q, k, v, kmean, batch, seq):
    n_blk = seq // MOBA_BLOCK
    n_hp = ATTN_HEADS // HEADS_PER_STEP
    q3 = q.reshape(batch, seq, ATTN_WIDTH)
    k3 = k.reshape(batch, seq, ATTN_WIDTH)
    v3 = v.reshape(batch, seq, ATTN_WIDTH)
    km3 = kmean.reshape(batch, n_blk, ATTN_WIDTH)
    q_spec = pl.BlockSpec((1, MOBA_BLOCK, V7X_LANES), lambda b, hp, i: (b, i, hp))
    kv_spec = pl.BlockSpec((1, seq, V7X_LANES), lambda b, hp, i: (b, 0, hp))
    out = pl.pallas_call(
        functools.partial(_moba_body, n_blk=n_blk),
        grid=(batch, n_hp, n_blk),
        in_specs=[q_spec, kv_spec, kv_spec,
                  pl.BlockSpec((1, n_blk, V7X_LANES), lambda b, hp, i: (b, 0, hp))],
        out_specs=q_spec,
        out_shape=jax.ShapeDtypeStruct((batch, seq, ATTN_WIDTH), BF16),
        scratch_shapes=[pltpu.VMEM((n_blk, HEADS_PER_STEP * MOBA_BLOCK), F32)],
        compiler_params=pltpu.CompilerParams(
            dimension_semantics=("parallel", "parallel", "arbitrary"),
            vmem_limit_bytes=V7X_VMEM_LIMIT_BYTES),
        name="moba",
    )(q3, k3, v3, km3)
    return out.reshape(batch * seq, ATTN_WIDTH)


def _mix_out_body(x_ref, attn_ref, rec_ref, gpost_ref, wo_ref, o_ref):
    mix = _dot(attn_ref[...], wo_ref[:ATTN_WIDTH, :]) + _dot(rec_ref[...], wo_ref[ATTN_WIDTH:, :])
    o_ref[...] = x_ref[...] + _rms(mix, gpost_ref[...])


def _mix_out(x, attn, rec, gpost, w_out, layer):
    m, d = x.shape
    row_spec = pl.BlockSpec((ROW_TILE, d), lambda i: (i, 0))
    half_spec = pl.BlockSpec((ROW_TILE, ATTN_WIDTH), lambda i: (i, 0))
    return pl.pallas_call(
        _mix_out_body,
        grid=(m // ROW_TILE,),
        in_specs=[row_spec, half_spec, half_spec, _resident((1, d), lambda i: (0, 0)),
                  _resident((None, ATTN_WIDTH + LRU_WIDTH, d), lambda i: (layer, 0, 0))],
        out_specs=row_spec,
        out_shape=jax.ShapeDtypeStruct((m, d), F32),
        compiler_params=pltpu.CompilerParams(
            dimension_semantics=("parallel",), vmem_limit_bytes=V7X_VMEM_LIMIT_BYTES),
        name="mix_out",
    )(x, attn, rec, gpost, w_out)


def _rope_tables(seq):
    half = ROT_DIM // 2
    pos = jnp.arange(seq, dtype=F32)
    inv_freq = ROPE_THETA ** (-jnp.arange(0, ROT_DIM, 2, dtype=F32) / ROT_DIM)
    ang = pos[:, None] * inv_freq[None, :]
    cos, sin = jnp.cos(ang), jnp.sin(ang)
    ones = jnp.ones((seq, HEAD_DIM - ROT_DIM), F32)
    zeros = jnp.zeros((seq, HEAD_DIM - ROT_DIM), F32)
    zhalf = jnp.zeros((seq, half), F32)
    cos_h = jnp.concatenate([cos, cos, ones], axis=1)
    sup_h = jnp.concatenate([-sin, zhalf, zeros], axis=1)
    sdn_h = jnp.concatenate([zhalf, sin, zeros], axis=1)
    rep = V7X_LANES // HEAD_DIM
    return jnp.tile(cos_h, (1, rep)), jnp.tile(sup_h, (1, rep)), jnp.tile(sdn_h, (1, rep))


def _block_diag(w):
    n, h, d, _ = w.shape
    eye = jnp.eye(h, dtype=w.dtype)
    return jnp.einsum("nhij,hg->nhigj", w, eye).reshape(n, h * d, h * d)


def kernel(x, ffn1_w_gate, ffn1_w_up, ffn1_w_down, ffn2_w_gate, ffn2_w_up, ffn2_w_down, norm_pre, norm_post,
           ab_w_in, ab_w_out, lru_conv_w, lru_conv_b, lru_gate_a_w, lru_gate_a_b, lru_gate_x_w, lru_gate_x_b,
           lru_lambda, c_w_in, c_conv_w, c_w_out):
    batch, seq, d = x.shape
    depth = norm_pre.shape[0]
    assert d == D_MODEL and seq % ROW_TILE == 0 and ROW_TILE % MOBA_BLOCK == 0
    assert seq // MOBA_BLOCK > MOBA_TOPK

    ffn1 = [w.astype(BF16) for w in (ffn1_w_gate, ffn1_w_up, ffn1_w_down)]
    ffn2 = [w.astype(BF16) for w in (ffn2_w_gate, ffn2_w_up, ffn2_w_down)]
    ab_in, ab_out = ab_w_in.astype(BF16), ab_w_out.astype(BF16)
    c_in, c_out = c_w_in.astype(BF16), c_w_out.astype(BF16)
    wa = _block_diag(lru_gate_a_w).astype(BF16)
    wx = _block_diag(lru_gate_x_w).astype(BF16)
    n_even = ab_w_in.shape[0]
    ba = lru_gate_a_b.reshape(n_even, 1, LRU_WIDTH)
    bx = lru_gate_x_b.reshape(n_even, 1, LRU_WIDTH)
    lam = lru_lambda.reshape(n_even, 1, LRU_WIDTH)
    conv_b = lru_conv_b.reshape(n_even, 1, LRU_WIDTH)
    cos, sup, sdn = _rope_tables(seq)

    xs = x.reshape(batch * seq, d)
    for l in range(depth):
        pre = lambda j: norm_pre[l, j].reshape(1, d)
        post = lambda j: norm_post[l, j].reshape(1, d)
        xs = _ffn(xs, pre(0), post(0), *ffn1, l)
        if l % 2 == 0:
            e = l // 2
            q, k, v, kmean, rec = _proj_lru(xs, pre(1), ab_in, cos, sup, sdn, lru_conv_w, conv_b,
                                            wa, wx, ba, bx, lam, e, batch, seq)
            attn = _moba(q, k, v, kmean, batch, seq)
            xs = _mix_out(xs, attn, rec, post(1), ab_out, e)
        else:
            xs = _sconv(xs, pre(1), post(1), c_in, c_conv_w, c_out, l // 2, batch, seq)
        xs = _ffn(xs, pre(2), post(2), *ffn2, l)
    return xs.reshape(batch, seq, d)
```

```python
import functools

import jax
import jax.numpy as jnp
from jax import lax
from jax.experimental import pallas as pl
from jax.experimental.pallas import tpu as pltpu

F32 = jnp.float32
BF16 = jnp.bfloat16

NORM_EPS = 1e-6
D_MODEL = 1024
D_FF = 2816

ATTN_HEADS = 8
HEAD_DIM = 64
ATTN_WIDTH = ATTN_HEADS * HEAD_DIM
ROT_DIM = HEAD_DIM // 4
ROPE_THETA = 500000.0
MOBA_BLOCK = 256
MOBA_TOPK = 3
MASK_VALUE = -1e30

LRU_WIDTH = 512
LRU_HEADS = 8
LRU_CONV = 4
LRU_C = 8.0
AB_IN = 3 * ATTN_WIDTH + 2 * LRU_WIDTH

SC_CONV = 3

V7X_LANES = 128
V7X_SUBLANES = 8
V7X_MXU_DIM = 256
V7X_VMEM_LIMIT_BYTES = 56 * 1024 * 1024

ROW_TILE = 512
FFN_ROW_GROUPS = 2
FF_CHUNKS = (6 * V7X_MXU_DIM, 5 * V7X_MXU_DIM)
assert sum(FF_CHUNKS) == D_FF
HEADS_PER_LANE_GROUP = V7X_LANES // HEAD_DIM
MOBA_LANE_GROUPS = 2
LOG2_E = 1.4426950408889634


def _rms(x, g):
    return x * lax.rsqrt(jnp.mean(x * x, axis=-1, keepdims=True) + NORM_EPS) * g


def _dot(a, b):
    return jnp.dot(a, b, preferred_element_type=F32)


def _resident(shape, index_map):
    return pl.BlockSpec(shape, index_map, pipeline_mode=pl.Buffered(1))


def _shift_rows(cur, tail, k):
    sh = pltpu.roll(cur, k, axis=0)
    tl = pltpu.roll(tail, k, axis=0)
    row = lax.broadcasted_iota(jnp.int32, (V7X_SUBLANES, 1), 0)
    head = jnp.where(row < k, tl, sh[:V7X_SUBLANES])
    return jnp.concatenate([head, sh[V7X_SUBLANES:]], axis=0)


def _ffn_body(x_ref, gpre_ref, gpost_ref, wg_ref, wu_ref, wd_ref, o_ref):
    rows = ROW_TILE // FFN_ROW_GROUPS
    for r in range(FFN_ROW_GROUPS):
        rs = slice(r * rows, (r + 1) * rows)
        x = x_ref[rs, :]
        h = _rms(x, gpre_ref[...]).astype(BF16)
        acc = None
        for c in range(len(FF_CHUNKS)):
            sl = slice(sum(FF_CHUNKS[:c]), sum(FF_CHUNKS[:c + 1]))
            g = _dot(h, wg_ref[:, sl])
            u = _dot(h, wu_ref[:, sl])
            a = (g * jax.nn.sigmoid(g) * u).astype(BF16)
            d = _dot(a, wd_ref[sl, :])
            acc = d if acc is None else acc + d
        o_ref[rs, :] = x + 0.5 * _rms(acc, gpost_ref[...])


def _ffn(x, gpre, gpost, wg, wu, wd, layer):
    m, d = x.shape
    row_spec = pl.BlockSpec((ROW_TILE, d), lambda i: (i, 0))
    vec_spec = _resident((1, d), lambda i: (0, 0))
    return pl.pallas_call(
        _ffn_body,
        grid=(m // ROW_TILE,),
        in_specs=[
            row_spec, vec_spec, vec_spec,
            _resident((None, d, D_FF), lambda i: (layer, 0, 0)),
            _resident((None, d, D_FF), lambda i: (layer, 0, 0)),
            _resident((None, D_FF, d), lambda i: (layer, 0, 0)),
        ],
        out_specs=row_spec,
        out_shape=jax.ShapeDtypeStruct((m, d), F32),
        compiler_params=pltpu.CompilerParams(
            dimension_semantics=("parallel",), vmem_limit_bytes=V7X_VMEM_LIMIT_BYTES),
        name="ffn",
    )(x, gpre, gpost, wg, wu, wd)


def _sconv_body(x_ref, gpre_ref, gpost_ref, win_ref, cw_ref, wout_ref, o_ref, tail_ref):
    @pl.when(pl.program_id(1) == 0)
    def _():
        tail_ref[...] = jnp.zeros_like(tail_ref)

    d = x_ref.shape[-1]
    x = x_ref[...]
    h = _rms(x, gpre_ref[...]).astype(BF16)
    p = _dot(h, win_ref[...])
    gate_b = p[:, :d]
    cx = p[:, d:2 * d] * p[:, 2 * d:]
    tail = tail_ref[...]
    conv = cw_ref[SC_CONV - 1:SC_CONV, :] * cx
    for k in range(1, SC_CONV):
        conv = conv + cw_ref[SC_CONV - 1 - k:SC_CONV - k, :] * _shift_rows(cx, tail, k)
    tail_ref[...] = cx[ROW_TILE - V7X_SUBLANES:, :]
    mix = _dot((gate_b * conv).astype(BF16), wout_ref[...])
    o_ref[...] = x + _rms(mix, gpost_ref[...])


def _sconv(x, gpre, gpost, w_in, conv_w, w_out, layer, batch, seq):
    m, d = x.shape
    n_s = seq // ROW_TILE
    row_spec = pl.BlockSpec((ROW_TILE, d), lambda b, s: (b * n_s + s, 0))
    vec_spec = _resident((1, d), lambda b, s: (0, 0))
    return pl.pallas_call(
        _sconv_body,
        grid=(batch, n_s),
        in_specs=[
            row_spec, vec_spec, vec_spec,
            _resident((None, d, 3 * d), lambda b, s: (layer, 0, 0)),
            _resident((None, SC_CONV, d), lambda b, s: (layer, 0, 0)),
            _resident((None, d, d), lambda b, s: (layer, 0, 0)),
        ],
        out_specs=row_spec,
        out_shape=jax.ShapeDtypeStruct((m, d), F32),
        scratch_shapes=[pltpu.VMEM((V7X_SUBLANES, d), F32)],
        compiler_params=pltpu.CompilerParams(
            dimension_semantics=("arbitrary", "arbitrary"), vmem_limit_bytes=V7X_VMEM_LIMIT_BYTES),
        name="sconv",
    )(x, gpre, gpost, w_in, conv_w, w_out)


def _rope(z, cos, sin_up, sin_dn):
    half = ROT_DIM // 2
    outs = []
    for c in range(ATTN_WIDTH // V7X_LANES):
        zc = z[:, c * V7X_LANES:(c + 1) * V7X_LANES]
        up = pltpu.roll(zc, V7X_LANES - half, axis=1)
        dn = pltpu.roll(zc, half, axis=1)
        outs.append(zc * cos + up * sin_up + dn * sin_dn)
    return jnp.concatenate(outs, axis=1)


def _row_groups(z):
    return z.reshape(z.shape[0] // V7X_SUBLANES, V7X_SUBLANES, z.shape[1])


def _shift_rows_grouped(cur, tail, k):
    sub = lax.broadcasted_iota(jnp.int32, (1, V7X_SUBLANES, 1), 1)
    rot = pltpu.roll(cur, k, axis=1)
    prev = jnp.concatenate([pltpu.roll(tail, k, axis=1), rot[:-1]], axis=0)
    return jnp.where(sub >= k, rot, prev)


def _scan_rows(a, u, h_in):
    sub = lax.broadcasted_iota(jnp.int32, (1, V7X_SUBLANES, 1), 1)
    d = 1
    while d < V7X_SUBLANES:
        keep = sub >= d
        a_s = jnp.where(keep, pltpu.roll(a, d, axis=1), 1.0)
        u_s = jnp.where(keep, pltpu.roll(u, d, axis=1), 0.0)
        u = a * u_s + u
        a = a * a_s
        d *= 2
    h = h_in
    out = []
    for g in range(a.shape[0]):
        hg = u[g] + a[g] * h
        out.append(hg)
        h = hg[V7X_SUBLANES - 1:, :]
    return jnp.concatenate(out, axis=0), h


def _proj_lru_body(x_ref, gpre_ref, win_ref, cos_ref, sup_ref, sdn_ref, cw_ref, cb_ref,
                   wa_ref, wx_ref, ba_ref, bx_ref, lam_ref,
                   q_ref, k_ref, v_ref, km_ref, rec_ref, xtail_ref, hprev_ref):
    @pl.when(pl.program_id(1) == 0)
    def _():
        xtail_ref[...] = jnp.zeros_like(xtail_ref)
        hprev_ref[...] = jnp.zeros_like(hprev_ref)

    w = ATTN_WIDTH
    xtail = xtail_ref[...]
    h_state = hprev_ref[...]
    softplus_neg_lam = jax.nn.softplus(-lam_ref[...])
    for g in range(ROW_TILE // MOBA_BLOCK):
        rs = slice(g * MOBA_BLOCK, (g + 1) * MOBA_BLOCK)
        h = _rms(x_ref[rs, :], gpre_ref[...]).astype(BF16)
        p = _dot(h, win_ref[...])

        cos, sup, sdn = cos_ref[rs, :], sup_ref[rs, :], sdn_ref[rs, :]
        q_ref[rs, :] = (_rope(p[:, :w], cos, sup, sdn) * (HEAD_DIM ** -0.5 * LOG2_E)).astype(BF16)
        k = _rope(p[:, w:2 * w], cos, sup, sdn)
        k_ref[rs, :] = k.astype(BF16)
        v_ref[rs, :] = p[:, 2 * w:3 * w].astype(BF16)
        km_ref[g] = jnp.mean(k, axis=0, keepdims=True)

        lx = p[:, 3 * w:3 * w + LRU_WIDTH]
        lg = p[:, 3 * w + LRU_WIDTH:]
        lx3 = _row_groups(lx)
        xc = cw_ref[LRU_CONV - 1:LRU_CONV, :] * lx + cb_ref[...]
        for kk in range(1, LRU_CONV):
            shifted = _shift_rows_grouped(lx3, xtail, kk).reshape(lx.shape)
            xc = xc + cw_ref[LRU_CONV - 1 - kk:LRU_CONV - kk, :] * shifted
        xtail = lx3[-1:]

        xcb = xc.astype(BF16)
        r = jax.nn.sigmoid(_dot(xcb, wa_ref[...]) + ba_ref[...])
        i = jax.nn.sigmoid(_dot(xcb, wx_ref[...]) + bx_ref[...])
        a = jnp.exp((-LRU_C) * r * softplus_neg_lam)
        z = 1.0 - a * a
        root = jnp.where(z > 0.0, z * lax.rsqrt(z), 0.0)
        u = root * (i * xc)
        hs, h_state = _scan_rows(_row_groups(a), _row_groups(u), h_state)
        rec_ref[rs, :] = (hs * jax.nn.gelu(lg)).astype(BF16)
    xtail_ref[...] = xtail
    hprev_ref[...] = h_state


def _proj_lru(x, gpre, w_in, cos, sup, sdn, conv_w, conv_b, wa, wx, ba, bx, lam, layer, batch, seq):
    m, d = x.shape
    n_s = seq // ROW_TILE
    blocks_per_tile = ROW_TILE // MOBA_BLOCK

    def rows(width):
        return pl.BlockSpec((ROW_TILE, width), lambda b, s: (b * n_s + s, 0))

    def res(shape):
        zeros = (0,) * (len(shape) - 1)
        return _resident(shape, lambda b, s: (layer,) + zeros)

    tab_spec = pl.BlockSpec((ROW_TILE, V7X_LANES), lambda b, s: (s, 0))
    return pl.pallas_call(
        _proj_lru_body,
        grid=(batch, n_s),
        in_specs=[
            rows(d), _resident((1, d), lambda b, s: (0, 0)),
            res((None, d, AB_IN)),
            tab_spec, tab_spec, tab_spec,
            res((None, LRU_CONV, LRU_WIDTH)), res((None, 1, LRU_WIDTH)),
            res((None, LRU_WIDTH, LRU_WIDTH)), res((None, LRU_WIDTH, LRU_WIDTH)),
            res((None, 1, LRU_WIDTH)), res((None, 1, LRU_WIDTH)), res((None, 1, LRU_WIDTH)),
        ],
        out_specs=[
            rows(ATTN_WIDTH), rows(ATTN_WIDTH), rows(ATTN_WIDTH),
            pl.BlockSpec((blocks_per_tile, 1, ATTN_WIDTH), lambda b, s: (b * n_s + s, 0, 0)),
            rows(LRU_WIDTH),
        ],
        out_shape=[
            jax.ShapeDtypeStruct((m, ATTN_WIDTH), BF16),
            jax.ShapeDtypeStruct((m, ATTN_WIDTH), BF16),
            jax.ShapeDtypeStruct((m, ATTN_WIDTH), BF16),
            jax.ShapeDtypeStruct((m // MOBA_BLOCK, 1, ATTN_WIDTH), F32),
            jax.ShapeDtypeStruct((m, LRU_WIDTH), BF16),
        ],
        scratch_shapes=[pltpu.VMEM((1, V7X_SUBLANES, LRU_WIDTH), F32),
                        pltpu.VMEM((1, LRU_WIDTH), F32)],
        compiler_params=pltpu.CompilerParams(
            dimension_semantics=("arbitrary", "arbitrary"), vmem_limit_bytes=V7X_VMEM_LIMIT_BYTES),
        name="proj_lru",
    )(x, gpre, w_in, cos, sup, sdn, conv_w, conv_b, wa, wx, ba, bx, lam)


_NT = (((1,), (1,)), ((), ()))
_TN = (((0,), (0,)), ((), ()))


def _block_gate_bias(q2, km, qi, n_blk):
    nq = q2.shape[0]
    km_hi = km.astype(BF16).astype(F32)
    km_mid = (km - km_hi).astype(BF16).astype(F32)
    km_lo = (km - km_hi - km_mid).astype(BF16).astype(F32)
    km3 = jnp.concatenate([km_hi, km_mid, km_lo], axis=0).astype(BF16)
    g3 = lax.dot_general(km3, q2, _NT, preferred_element_type=F32)
    g = g3[:n_blk] + g3[n_blk:2 * n_blk] + g3[2 * n_blk:]
    bid = lax.broadcasted_iota(jnp.int32, (n_blk, nq), 0)
    g = jnp.where(bid < qi, g, MASK_VALUE)
    bias = jnp.full((n_blk, nq), MASK_VALUE, F32)
    for j in range(n_blk):
        gj = g[j:j + 1, :]
        beats = (g > gj) | ((g == gj) & (bid < j))
        rank = jnp.sum(beats.astype(jnp.int32), axis=0, keepdims=True)
        chosen = (rank < MOBA_TOPK) & (j < qi)
        bias = jnp.where((bid == j) & chosen, 0.0, bias)
    return bias


def _moba_body(q_ref, k_ref, v_ref, km_ref, o_ref, *, n_blk):
    qi = pl.program_id(2)
    blk = MOBA_BLOCK
    nq = HEADS_PER_LANE_GROUP * blk
    lane = lax.broadcasted_iota(jnp.int32, (1, V7X_LANES), 1)
    kpos = lax.broadcasted_iota(jnp.int32, (blk, nq), 0)
    qpos = lax.broadcasted_iota(jnp.int32, (blk, nq), 1) % blk
    causal = kpos <= qpos

    def attend(n, c):
        rows = n * blk
        lanes = slice(c * V7X_LANES, (c + 1) * V7X_LANES)
        q = q_ref[0, :, lanes]
        zero = jnp.zeros_like(q)
        q2 = jnp.concatenate(
            [jnp.where((lane // HEAD_DIM) == hh, q, zero) for hh in range(HEADS_PER_LANE_GROUP)], axis=0)
        bias = _block_gate_bias(q2, km_ref[0, :, lanes], qi, n_blk)
        s = lax.dot_general(k_ref[0, :rows, lanes], q2, _NT, preferred_element_type=F32)
        parts = [s[j * blk:(j + 1) * blk] + bias[j:j + 1, :] for j in range(n - 1)]
        parts.append(jnp.where(causal, s[rows - blk:], MASK_VALUE))
        s = jnp.concatenate(parts, axis=0)
        p = jnp.exp2(s - jnp.max(s, axis=0, keepdims=True))
        l = jnp.sum(p, axis=0, keepdims=True)
        acc = lax.dot_general(v_ref[0, :rows, lanes], p.astype(BF16), _TN, preferred_element_type=F32)
        out_t = acc / l
        out_t = jnp.concatenate(
            [out_t[hh * HEAD_DIM:(hh + 1) * HEAD_DIM, hh * blk:(hh + 1) * blk]
             for hh in range(HEADS_PER_LANE_GROUP)], axis=0)
        o_ref[0, :, lanes] = out_t.T.astype(BF16)

    def variant(n):
        for c in range(MOBA_LANE_GROUPS):
            attend(n, c)

    for n in range(1, n_blk + 1):
        pl.when(qi == n - 1)(functools.partial(variant, n))


def _moba(q, k, v, kmean, batch, seq):
    n_blk = seq // MOBA_BLOCK
    width = MOBA_LANE_GROUPS * V7X_LANES
    q3 = q.reshape(batch, seq, ATTN_WIDTH)
    k3 = k.reshape(batch, seq, ATTN_WIDTH)
    v3 = v.reshape(batch, seq, ATTN_WIDTH)
    km3 = kmean.reshape(batch, n_blk, ATTN_WIDTH)
    q_spec = pl.BlockSpec((1, MOBA_BLOCK, width), lambda b, hg, i: (b, i, hg))
    kv_spec = pl.BlockSpec((1, seq, width), lambda b, hg, i: (b, 0, hg))
    out = pl.pallas_call(
        functools.partial(_moba_body, n_blk=n_blk),
        grid=(batch, ATTN_WIDTH // width, n_blk),
        in_specs=[q_spec, kv_spec, kv_spec,
                  pl.BlockSpec((1, n_blk, width), lambda b, hg, i: (b, 0, hg))],
        out_specs=q_spec,
        out_shape=jax.ShapeDtypeStruct((batch, seq, ATTN_WIDTH), BF16),
        compiler_params=pltpu.CompilerParams(
            dimension_semantics=("parallel", "parallel", "arbitrary"),
            vmem_limit_bytes=V7X_VMEM_LIMIT_BYTES),
        name="moba",
    )(q3, k3, v3, km3)
    return out.reshape(batch * seq, ATTN_WIDTH)


def _mix_out_body(x_ref, attn_ref, rec_ref, gpost_ref, wo_ref, o_ref):
    mix = _dot(attn_ref[...], wo_ref[:ATTN_WIDTH, :]) + _dot(rec_ref[...], wo_ref[ATTN_WIDTH:, :])
    o_ref[...] = x_ref[...] + _rms(mix, gpost_ref[...])


def _mix_out(x, attn, rec, gpost, w_out, layer):
    m, d = x.shape
    row_spec = pl.BlockSpec((ROW_TILE, d), lambda i: (i, 0))
    half_spec = pl.BlockSpec((ROW_TILE, ATTN_WIDTH), lambda i: (i, 0))
    return pl.pallas_call(
        _mix_out_body,
        grid=(m // ROW_TILE,),
        in_specs=[row_spec, half_spec, half_spec, _resident((1, d), lambda i: (0, 0)),
                  _resident((None, ATTN_WIDTH + LRU_WIDTH, d), lambda i: (layer, 0, 0))],
        out_specs=row_spec,
        out_shape=jax.ShapeDtypeStruct((m, d), F32),
        compiler_params=pltpu.CompilerParams(
            dimension_semantics=("parallel",), vmem_limit_bytes=V7X_VMEM_LIMIT_BYTES),
        name="mix_out",
    )(x, attn, rec, gpost, w_out)


def _rope_tables(seq):
    half = ROT_DIM // 2
    pos = jnp.arange(seq, dtype=F32)
    inv_freq = ROPE_THETA ** (-jnp.arange(0, ROT_DIM, 2, dtype=F32) / ROT_DIM)
    ang = pos[:, None] * inv_freq[None, :]
    cos, sin = jnp.cos(ang), jnp.sin(ang)
    ones = jnp.ones((seq, HEAD_DIM - ROT_DIM), F32)
    zeros = jnp.zeros((seq, HEAD_DIM - ROT_DIM), F32)
    zhalf = jnp.zeros((seq, half), F32)
    cos_h = jnp.concatenate([cos, cos, ones], axis=1)
    sup_h = jnp.concatenate([-sin, zhalf, zeros], axis=1)
    sdn_h = jnp.concatenate([zhalf, sin, zeros], axis=1)
    rep = V7X_LANES // HEAD_DIM
    return jnp.tile(cos_h, (1, rep)), jnp.tile(sup_h, (1, rep)), jnp.tile(sdn_h, (1, rep))


def _block_diag(w):
    n, h, d, _ = w.shape
    eye = jnp.eye(h, dtype=w.dtype)
    return jnp.einsum("nhij,hg->nhigj", w, eye).reshape(n, h * d, h * d)


def kernel(x, ffn1_w_gate, ffn1_w_up, ffn1_w_down, ffn2_w_gate, ffn2_w_up, ffn2_w_down, norm_pre, norm_post,
           ab_w_in, ab_w_out, lru_conv_w, lru_conv_b, lru_gate_a_w, lru_gate_a_b, lru_gate_x_w, lru_gate_x_b,
           lru_lambda, c_w_in, c_conv_w, c_w_out):
    batch, seq, d = x.shape
    depth = norm_pre.shape[0]
    assert d == D_MODEL and seq % ROW_TILE == 0 and ROW_TILE % MOBA_BLOCK == 0
    assert seq // MOBA_BLOCK > MOBA_TOPK

    ffn1 = [w.astype(BF16) for w in (ffn1_w_gate, ffn1_w_up, ffn1_w_down)]
    ffn2 = [w.astype(BF16) for w in (ffn2_w_gate, ffn2_w_up, ffn2_w_down)]
    ab_in, ab_out = ab_w_in.astype(BF16), ab_w_out.astype(BF16)
    c_in, c_out = c_w_in.astype(BF16), c_w_out.astype(BF16)
    wa = _block_diag(lru_gate_a_w).astype(BF16)
    wx = _block_diag(lru_gate_x_w).astype(BF16)
    n_even = ab_w_in.shape[0]
    ba = lru_gate_a_b.reshape(n_even, 1, LRU_WIDTH)
    bx = lru_gate_x_b.reshape(n_even, 1, LRU_WIDTH)
    lam = lru_lambda.reshape(n_even, 1, LRU_WIDTH)
    conv_b = lru_conv_b.reshape(n_even, 1, LRU_WIDTH)
    cos, sup, sdn = _rope_tables(seq)

    xs = x.reshape(batch * seq, d)
    for l in range(depth):
        pre = lambda j: norm_pre[l, j].reshape(1, d)
        post = lambda j: norm_post[l, j].reshape(1, d)
        xs = _ffn(xs, pre(0), post(0), *ffn1, l)
        if l % 2 == 0:
            e = l // 2
            q, k, v, kmean, rec = _proj_lru(xs, pre(1), ab_in, cos, sup, sdn, lru_conv_w, conv_b,
                                            wa, wx, ba, bx, lam, e, batch, seq)
            attn = _moba(q, k, v, kmean, batch, seq)
            xs = _mix_out(xs, attn, rec, post(1), ab_out, e)
        else:
            xs = _sconv(xs, pre(1), post(1), c_in, c_conv_w, c_out, l // 2, batch, seq)
        xs = _ffn(xs, pre(2), post(2), *ffn2, l)
    return xs.reshape(batch, seq, d)
```

```python
import functools

import jax
import jax.numpy as jnp
from jax import lax
from jax.experimental import pallas as pl
from jax.experimental.pallas import tpu as pltpu

F32 = jnp.float32
BF16 = jnp.bfloat16

NORM_EPS = 1e-6
D_MODEL = 1024
D_FF = 2816

ATTN_HEADS = 8
HEAD_DIM = 64
ATTN_WIDTH = ATTN_HEADS * HEAD_DIM
ROT_DIM = HEAD_DIM // 4
ROPE_THETA = 500000.0
MOBA_BLOCK = 256
MOBA_TOPK = 3
MASK_VALUE = -1e30

LRU_WIDTH = 512
LRU_HEADS = 8
LRU_CONV = 4
LRU_C = 8.0
AB_IN = 3 * ATTN_WIDTH + 2 * LRU_WIDTH

SC_CONV = 3

V7X_LANES = 128
V7X_SUBLANES = 8
V7X_MXU_DIM = 256
V7X_VMEM_LIMIT_BYTES = 56 * 1024 * 1024

ROW_TILE = 512
FFN_ROW_GROUPS = 2
FF_CHUNKS = (6 * V7X_MXU_DIM, 5 * V7X_MXU_DIM)
assert sum(FF_CHUNKS) == D_FF
HEADS_PER_LANE_GROUP = V7X_LANES // HEAD_DIM
MOBA_LANE_GROUPS = 4
LOG2_E = 1.4426950408889634


def _rms(x, g):
    return x * lax.rsqrt(jnp.mean(x * x, axis=-1, keepdims=True) + NORM_EPS) * g


def _dot(a, b):
    return jnp.dot(a, b, preferred_element_type=F32)


def _resident(shape, index_map):
    return pl.BlockSpec(shape, index_map, pipeline_mode=pl.Buffered(1))


def _half_ffn_step(x, gpre_ref, gpost_ref, wg_ref, wu_ref, wd_ref):
    h = _rms(x, gpre_ref[...]).astype(BF16)
    acc = None
    for c in range(len(FF_CHUNKS)):
        sl = slice(sum(FF_CHUNKS[:c]), sum(FF_CHUNKS[:c + 1]))
        g = _dot(h, wg_ref[:, sl])
        u = _dot(h, wu_ref[:, sl])
        a = (g * jax.nn.sigmoid(g) * u).astype(BF16)
        d = _dot(a, wd_ref[sl, :])
        acc = d if acc is None else acc + d
    return x + 0.5 * _rms(acc, gpost_ref[...])


def _ffn_row_groups():
    rows = ROW_TILE // FFN_ROW_GROUPS
    return [slice(r * rows, (r + 1) * rows) for r in range(FFN_ROW_GROUPS)]


def _ffn_body(x_ref, gpre_ref, gpost_ref, wg_ref, wu_ref, wd_ref, o_ref):
    for rs in _ffn_row_groups():
        o_ref[rs, :] = _half_ffn_step(x_ref[rs, :], gpre_ref, gpost_ref, wg_ref, wu_ref, wd_ref)


def _mix_ffn_body(x_ref, attn_ref, rec_ref, gmix_ref, wo_ref, gpre_ref, gpost_ref, wg_ref, wu_ref, wd_ref, o_ref):
    xs = []
    for rs in _ffn_row_groups():
        mix = _dot(attn_ref[rs, :], wo_ref[:ATTN_WIDTH, :]) + _dot(rec_ref[rs, :], wo_ref[ATTN_WIDTH:, :])
        xs.append(x_ref[rs, :] + _rms(mix, gmix_ref[...]))
    for rs, x in zip(_ffn_row_groups(), xs):
        o_ref[rs, :] = _half_ffn_step(x, gpre_ref, gpost_ref, wg_ref, wu_ref, wd_ref)


def _ffn(x, gpre, gpost, wg, wu, wd, layer, mixer=None):
    m, d = x.shape
    row_spec = pl.BlockSpec((ROW_TILE, d), lambda i: (i, 0))
    vec_spec = _resident((1, d), lambda i: (0, 0))
    ffn_specs = [
        vec_spec, vec_spec,
        _resident((None, d, D_FF), lambda i: (layer, 0, 0)),
        _resident((None, d, D_FF), lambda i: (layer, 0, 0)),
        _resident((None, D_FF, d), lambda i: (layer, 0, 0)),
    ]
    if mixer is None:
        body, name = _ffn_body, "ffn"
        in_specs = [row_spec] + ffn_specs
        args = (x, gpre, gpost, wg, wu, wd)
    else:
        attn, rec, gmix, w_out, mixer_layer = mixer
        body, name = _mix_ffn_body, "mix_ffn"
        half_spec = pl.BlockSpec((ROW_TILE, ATTN_WIDTH), lambda i: (i, 0))
        in_specs = [row_spec, half_spec, half_spec, vec_spec,
                    _resident((None, ATTN_WIDTH + LRU_WIDTH, d), lambda i: (mixer_layer, 0, 0))] + ffn_specs
        args = (x, attn, rec, gmix, w_out, gpre, gpost, wg, wu, wd)
    return pl.pallas_call(
        body,
        grid=(m // ROW_TILE,),
        in_specs=in_specs,
        out_specs=row_spec,
        out_shape=jax.ShapeDtypeStruct((m, d), F32),
        compiler_params=pltpu.CompilerParams(
            dimension_semantics=("parallel",), vmem_limit_bytes=V7X_VMEM_LIMIT_BYTES),
        name=name,
    )(*args)


def _sconv_body(x_ref, gpre_ref, gpost_ref, win_ref, cw_ref, wout_ref, o_ref, tail_ref):
    @pl.when(pl.program_id(1) == 0)
    def _():
        tail_ref[...] = jnp.zeros_like(tail_ref)

    d = x_ref.shape[-1]
    tail = tail_ref[...]
    for rs in [slice(0, ROW_TILE)]:
        x = x_ref[rs, :]
        h = _rms(x, gpre_ref[...]).astype(BF16)
        p = _dot(h, win_ref[...])
        gate_b = p[:, :d]
        cx = p[:, d:2 * d] * p[:, 2 * d:]
        cx3 = _row_groups(cx)
        conv = cw_ref[SC_CONV - 1:SC_CONV, :] * cx
        for k in range(1, SC_CONV):
            conv = conv + cw_ref[SC_CONV - 1 - k:SC_CONV - k, :] * _shift_rows_grouped(cx3, tail, k).reshape(cx.shape)
        tail = cx3[-1:]
        mix = _dot((gate_b * conv).astype(BF16), wout_ref[...])
        o_ref[rs, :] = x + _rms(mix, gpost_ref[...])
    tail_ref[...] = tail


def _sconv(x, gpre, gpost, w_in, conv_w, w_out, layer, batch, seq):
    m, d = x.shape
    n_s = seq // ROW_TILE
    row_spec = pl.BlockSpec((ROW_TILE, d), lambda b, s: (b * n_s + s, 0))
    vec_spec = _resident((1, d), lambda b, s: (0, 0))
    return pl.pallas_call(
        _sconv_body,
        grid=(batch, n_s),
        in_specs=[
            row_spec, vec_spec, vec_spec,
            _resident((None, d, 3 * d), lambda b, s: (layer, 0, 0)),
            _resident((None, SC_CONV, d), lambda b, s: (layer, 0, 0)),
            _resident((None, d, d), lambda b, s: (layer, 0, 0)),
        ],
        out_specs=row_spec,
        out_shape=jax.ShapeDtypeStruct((m, d), F32),
        scratch_shapes=[pltpu.VMEM((1, V7X_SUBLANES, d), F32)],
        compiler_params=pltpu.CompilerParams(
            dimension_semantics=("arbitrary", "arbitrary"), vmem_limit_bytes=V7X_VMEM_LIMIT_BYTES),
        name="sconv",
    )(x, gpre, gpost, w_in, conv_w, w_out)


def _rope(z, cos, sin_up, sin_dn):
    half = ROT_DIM // 2
    outs = []
    for c in range(ATTN_WIDTH // V7X_LANES):
        zc = z[:, c * V7X_LANES:(c + 1) * V7X_LANES]
        up = pltpu.roll(zc, V7X_LANES - half, axis=1)
        dn = pltpu.roll(zc, half, axis=1)
        outs.append(zc * cos + up * sin_up + dn * sin_dn)
    return jnp.concatenate(outs, axis=1)


def _row_groups(z):
    return z.reshape(z.shape[0] // V7X_SUBLANES, V7X_SUBLANES, z.shape[1])


def _shift_rows_grouped(cur, tail, k):
    sub = lax.broadcasted_iota(jnp.int32, (1, V7X_SUBLANES, 1), 1)
    rot = pltpu.roll(cur, k, axis=1)
    prev = jnp.concatenate([pltpu.roll(tail, k, axis=1), rot[:-1]], axis=0)
    return jnp.where(sub >= k, rot, prev)


def _scan_rows(a, u, h_in):
    sub = lax.broadcasted_iota(jnp.int32, (1, V7X_SUBLANES, 1), 1)
    d = 1
    while d < V7X_SUBLANES:
        keep = sub >= d
        a_s = jnp.where(keep, pltpu.roll(a, d, axis=1), 1.0)
        u_s = jnp.where(keep, pltpu.roll(u, d, axis=1), 0.0)
        u = a * u_s + u
        a = a * a_s
        d *= 2
    h = h_in
    out = []
    for g in range(a.shape[0]):
        hg = u[g] + a[g] * h
        out.append(hg)
        h = hg[V7X_SUBLANES - 1:, :]
    return jnp.concatenate(out, axis=0), h


def _proj_lru_body(x_ref, gpre_ref, win_ref, cos_ref, sup_ref, sdn_ref, cw_ref, cb_ref,
                   wa_ref, wx_ref, ba_ref, bx_ref, lam_ref,
                   q_ref, k_ref, v_ref, km_ref, rec_ref, xtail_ref, hprev_ref):
    @pl.when(pl.program_id(1) == 0)
    def _():
        xtail_ref[...] = jnp.zeros_like(xtail_ref)
        hprev_ref[...] = jnp.zeros_like(hprev_ref)

    w = ATTN_WIDTH
    xtail = xtail_ref[...]
    h_state = hprev_ref[...]
    softplus_neg_lam = jax.nn.softplus(-lam_ref[...])
    for g in range(ROW_TILE // MOBA_BLOCK):
        rs = slice(g * MOBA_BLOCK, (g + 1) * MOBA_BLOCK)
        h = _rms(x_ref[rs, :], gpre_ref[...]).astype(BF16)
        p_lru = _dot(h, win_ref[:, 3 * w:])
        p = _dot(h, win_ref[:, :3 * w])

        cos, sup, sdn = cos_ref[rs, :], sup_ref[rs, :], sdn_ref[rs, :]
        q_ref[rs, :] = (_rope(p[:, :w], cos, sup, sdn) * (HEAD_DIM ** -0.5 * LOG2_E)).astype(BF16)
        k = _rope(p[:, w:2 * w], cos, sup, sdn)
        k_ref[rs, :] = k.astype(BF16)
        v_ref[rs, :] = p[:, 2 * w:3 * w].astype(BF16)
        km_ref[g] = jnp.mean(k, axis=0, keepdims=True)

        lx = p_lru[:, :LRU_WIDTH]
        lg = p_lru[:, LRU_WIDTH:]
        lx3 = _row_groups(lx)
        xc = cw_ref[LRU_CONV - 1:LRU_CONV, :] * lx + cb_ref[...]
        for kk in range(1, LRU_CONV):
            shifted = _shift_rows_grouped(lx3, xtail, kk).reshape(lx.shape)
            xc = xc + cw_ref[LRU_CONV - 1 - kk:LRU_CONV - kk, :] * shifted
        xtail = lx3[-1:]

        xcb = xc.astype(BF16)
        r = jax.nn.sigmoid(_dot(xcb, wa_ref[...]) + ba_ref[...])
        i = jax.nn.sigmoid(_dot(xcb, wx_ref[...]) + bx_ref[...])
        a = jnp.exp((-LRU_C) * r * softplus_neg_lam)
        z = 1.0 - a * a
        root = jnp.where(z > 0.0, z * lax.rsqrt(z), 0.0)
        u = root * (i * xc)
        hs, h_state = _scan_rows(_row_groups(a), _row_groups(u), h_state)
        rec_ref[rs, :] = (hs * jax.nn.gelu(lg)).astype(BF16)
    xtail_ref[...] = xtail
    hprev_ref[...] = h_state


def _proj_lru(x, gpre, w_in, cos, sup, sdn, conv_w, conv_b, wa, wx, ba, bx, lam, layer, batch, seq):
    m, d = x.shape
    n_s = seq // ROW_TILE
    blocks_per_tile = ROW_TILE // MOBA_BLOCK

    def rows(width):
        return pl.BlockSpec((ROW_TILE, width), lambda b, s: (b * n_s + s, 0))

    def res(shape):
        zeros = (0,) * (len(shape) - 1)
        return _resident(shape, lambda b, s: (layer,) + zeros)

    tab_spec = pl.BlockSpec((ROW_TILE, V7X_LANES), lambda b, s: (s, 0))
    return pl.pallas_call(
        _proj_lru_body,
        grid=(batch, n_s),
        in_specs=[
            rows(d), _resident((1, d), lambda b, s: (0, 0)),
            res((None, d, AB_IN)),
            tab_spec, tab_spec, tab_spec,
            res((None, LRU_CONV, LRU_WIDTH)), res((None, 1, LRU_WIDTH)),
            res((None, LRU_WIDTH, LRU_WIDTH)), res((None, LRU_WIDTH, LRU_WIDTH)),
            res((None, 1, LRU_WIDTH)), res((None, 1, LRU_WIDTH)), res((None, 1, LRU_WIDTH)),
        ],
        out_specs=[
            rows(ATTN_WIDTH), rows(ATTN_WIDTH), rows(ATTN_WIDTH),
            pl.BlockSpec((blocks_per_tile, 1, ATTN_WIDTH), lambda b, s: (b * n_s + s, 0, 0)),
            rows(LRU_WIDTH),
        ],
        out_shape=[
            jax.ShapeDtypeStruct((m, ATTN_WIDTH), BF16),
            jax.ShapeDtypeStruct((m, ATTN_WIDTH), BF16),
            jax.ShapeDtypeStruct((m, ATTN_WIDTH), BF16),
            jax.ShapeDtypeStruct((m // MOBA_BLOCK, 1, ATTN_WIDTH), F32),
            jax.ShapeDtypeStruct((m, LRU_WIDTH), BF16),
        ],
        scratch_shapes=[pltpu.VMEM((1, V7X_SUBLANES, LRU_WIDTH), F32),
                        pltpu.VMEM((1, LRU_WIDTH), F32)],
        compiler_params=pltpu.CompilerParams(
            dimension_semantics=("arbitrary", "arbitrary"), vmem_limit_bytes=V7X_VMEM_LIMIT_BYTES),
        name="proj_lru",
    )(x, gpre, w_in, cos, sup, sdn, conv_w, conv_b, wa, wx, ba, bx, lam)


_NT = (((1,), (1,)), ((), ()))
_TN = (((0,), (0,)), ((), ()))


def _block_gate_bias(q2, km, qi, n_blk):
    nq = q2.shape[0]
    km_hi = km.astype(BF16).astype(F32)
    km_mid = (km - km_hi).astype(BF16).astype(F32)
    km_lo = (km - km_hi - km_mid).astype(BF16).astype(F32)
    km3 = jnp.concatenate([km_hi, km_mid, km_lo], axis=0).astype(BF16)
    g3 = lax.dot_general(km3, q2, _NT, preferred_element_type=F32)
    g = g3[:n_blk] + g3[n_blk:2 * n_blk] + g3[2 * n_blk:]
    bid = lax.broadcasted_iota(jnp.int32, (n_blk, nq), 0)
    g = jnp.where(bid < qi, g, MASK_VALUE)
    bias = jnp.full((n_blk, nq), MASK_VALUE, F32)
    for j in range(n_blk):
        gj = g[j:j + 1, :]
        beats = (g > gj) | ((g == gj) & (bid < j))
        rank = jnp.sum(beats.astype(jnp.int32), axis=0, keepdims=True)
        chosen = (rank < MOBA_TOPK) & (j < qi)
        bias = jnp.where((bid == j) & chosen, 0.0, bias)
    return bias


def _moba_body(q_ref, k_ref, v_ref, km_ref, o_ref, *, n_blk):
    qi = pl.program_id(2)
    blk = MOBA_BLOCK
    nq = HEADS_PER_LANE_GROUP * blk
    lane = lax.broadcasted_iota(jnp.int32, (1, V7X_LANES), 1)
    kpos = lax.broadcasted_iota(jnp.int32, (blk, nq), 0)
    qpos = lax.broadcasted_iota(jnp.int32, (blk, nq), 1) % blk
    causal = kpos <= qpos

    def variant(n):
        rows = n * blk

        def lanes(c):
            return slice(c * V7X_LANES, (c + 1) * V7X_LANES)

        def scores(c):
            q = q_ref[0, :, lanes(c)]
            zero = jnp.zeros_like(q)
            q2 = jnp.concatenate(
                [jnp.where((lane // HEAD_DIM) == hh, q, zero) for hh in range(HEADS_PER_LANE_GROUP)], axis=0)
            bias = _block_gate_bias(q2, km_ref[0, :, lanes(c)], qi, n_blk)
            return lax.dot_general(k_ref[0, :rows, lanes(c)], q2, _NT, preferred_element_type=F32), bias

        def probs(s, bias):
            own = jnp.where(causal, s[rows - blk:], MASK_VALUE)
            tops = [jnp.max(s[j * blk:(j + 1) * blk], axis=0, keepdims=True) + bias[j:j + 1, :]
                    for j in range(n - 1)]
            tops.append(jnp.max(own, axis=0, keepdims=True))
            m = functools.reduce(jnp.maximum, tops)
            parts = [jnp.exp2(s[j * blk:(j + 1) * blk] - (m - bias[j:j + 1, :])) for j in range(n - 1)]
            parts.append(jnp.exp2(own - m))
            p = jnp.concatenate(parts, axis=0)
            return p.astype(BF16), jnp.sum(p, axis=0, keepdims=True)

        def values(c, p, l):
            acc = lax.dot_general(v_ref[0, :rows, lanes(c)], p, _TN, preferred_element_type=F32)
            out_t = acc / l
            out_t = jnp.concatenate(
                [out_t[hh * HEAD_DIM:(hh + 1) * HEAD_DIM, hh * blk:(hh + 1) * blk]
                 for hh in range(HEADS_PER_LANE_GROUP)], axis=0)
            o_ref[0, :, lanes(c)] = out_t.T.astype(BF16)

        pending = [scores(0)]
        for c in range(MOBA_LANE_GROUPS):
            if c + 1 < MOBA_LANE_GROUPS:
                pending.append(scores(c + 1))
            values(c, *probs(*pending[c]))

    for n in range(1, n_blk + 1):
        pl.when(qi == n - 1)(functools.partial(variant, n))


def _moba(q, k, v, kmean, batch, seq):
    n_blk = seq // MOBA_BLOCK
    width = MOBA_LANE_GROUPS * V7X_LANES
    q3 = q.reshape(batch, seq, ATTN_WIDTH)
    k3 = k.reshape(batch, seq, ATTN_WIDTH)
    v3 = v.reshape(batch, seq, ATTN_WIDTH)
    km3 = kmean.reshape(batch, n_blk, ATTN_WIDTH)
    q_spec = pl.BlockSpec((1, MOBA_BLOCK, width), lambda b, hg, i: (b, i, hg))
    kv_spec = pl.BlockSpec((1, seq, width), lambda b, hg, i: (b, 0, hg))
    out = pl.pallas_call(
        functools.partial(_moba_body, n_blk=n_blk),
        grid=(batch, ATTN_WIDTH // width, n_blk),
        in_specs=[q_spec, kv_spec, kv_spec,
                  pl.BlockSpec((1, n_blk, width), lambda b, hg, i: (b, 0, hg))],
        out_specs=q_spec,
        out_shape=jax.ShapeDtypeStruct((batch, seq, ATTN_WIDTH), BF16),
        compiler_params=pltpu.CompilerParams(
            dimension_semantics=("parallel", "parallel", "arbitrary"),
            vmem_limit_bytes=V7X_VMEM_LIMIT_BYTES),
        name="moba",
    )(q3, k3, v3, km3)
    return out.reshape(batch * seq, ATTN_WIDTH)


def _rope_tables(seq):
    half = ROT_DIM // 2
    pos = jnp.arange(seq, dtype=F32)
    inv_freq = ROPE_THETA ** (-jnp.arange(0, ROT_DIM, 2, dtype=F32) / ROT_DIM)
    ang = pos[:, None] * inv_freq[None, :]
    cos, sin = jnp.cos(ang), jnp.sin(ang)
    ones = jnp.ones((seq, HEAD_DIM - ROT_DIM), F32)
    zeros = jnp.zeros((seq, HEAD_DIM - ROT_DIM), F32)
    zhalf = jnp.zeros((seq, half), F32)
    cos_h = jnp.concatenate([cos, cos, ones], axis=1)
    sup_h = jnp.concatenate([-sin, zhalf, zeros], axis=1)
    sdn_h = jnp.concatenate([zhalf, sin, zeros], axis=1)
    rep = V7X_LANES // HEAD_DIM
    return jnp.tile(cos_h, (1, rep)), jnp.tile(sup_h, (1, rep)), jnp.tile(sdn_h, (1, rep))


def _block_diag(w):
    n, h, d, _ = w.shape
    eye = jnp.eye(h, dtype=w.dtype)
    return jnp.einsum("nhij,hg->nhigj", w, eye).reshape(n, h * d, h * d)


def kernel(x, ffn1_w_gate, ffn1_w_up, ffn1_w_down, ffn2_w_gate, ffn2_w_up, ffn2_w_down, norm_pre, norm_post,
           ab_w_in, ab_w_out, lru_conv_w, lru_conv_b, lru_gate_a_w, lru_gate_a_b, lru_gate_x_w, lru_gate_x_b,
           lru_lambda, c_w_in, c_conv_w, c_w_out):
    batch, seq, d = x.shape
    depth = norm_pre.shape[0]
    assert d == D_MODEL and seq % ROW_TILE == 0 and ROW_TILE % MOBA_BLOCK == 0
    assert seq // MOBA_BLOCK > MOBA_TOPK

    ffn1 = [w.astype(BF16) for w in (ffn1_w_gate, ffn1_w_up, ffn1_w_down)]
    ffn2 = [w.astype(BF16) for w in (ffn2_w_gate, ffn2_w_up, ffn2_w_down)]
    ab_in, ab_out = ab_w_in.astype(BF16), ab_w_out.astype(BF16)
    c_in, c_out = c_w_in.astype(BF16), c_w_out.astype(BF16)
    wa = _block_diag(lru_gate_a_w).astype(BF16)
    wx = _block_diag(lru_gate_x_w).astype(BF16)
    n_even = ab_w_in.shape[0]
    ba = lru_gate_a_b.reshape(n_even, 1, LRU_WIDTH)
    bx = lru_gate_x_b.reshape(n_even, 1, LRU_WIDTH)
    lam = lru_lambda.reshape(n_even, 1, LRU_WIDTH)
    conv_b = lru_conv_b.reshape(n_even, 1, LRU_WIDTH)
    cos, sup, sdn = _rope_tables(seq)

    xs = x.reshape(batch * seq, d)
    for l in range(depth):
        pre = lambda j: norm_pre[l, j].reshape(1, d)
        post = lambda j: norm_post[l, j].reshape(1, d)
        xs = _ffn(xs, pre(0), post(0), *ffn1, l)
        mixer = None
        if l % 2 == 0:
            e = l // 2
            q, k, v, kmean, rec = _proj_lru(xs, pre(1), ab_in, cos, sup, sdn, lru_conv_w, conv_b,
                                            wa, wx, ba, bx, lam, e, batch, seq)
            attn = _moba(q, k, v, kmean, batch, seq)
            mixer = (attn, rec, post(1), ab_out, e)
        else:
            xs = _sconv(xs, pre(1), post(1), c_in, c_conv_w, c_out, l // 2, batch, seq)
        xs = _ffn(xs, pre(2), post(2), *ffn2, l, mixer=mixer)
    return xs.reshape(batch, seq, d)
```

```python
import functools

import jax
import jax.numpy as jnp
from jax import lax
from jax.experimental import pallas as pl
from jax.experimental.pallas import tpu as pltpu

F32 = jnp.float32
BF16 = jnp.bfloat16

NORM_EPS = 1e-6
D_MODEL = 1024
D_FF = 2816

ATTN_HEADS = 8
HEAD_DIM = 64
ATTN_WIDTH = ATTN_HEADS * HEAD_DIM
ROT_DIM = HEAD_DIM // 4
ROPE_THETA = 500000.0
MOBA_BLOCK = 256
MOBA_TOPK = 3
MASK_VALUE = -1e30

LRU_WIDTH = 512
LRU_HEADS = 8
LRU_CONV = 4
LRU_C = 8.0
AB_IN = 3 * ATTN_WIDTH + 2 * LRU_WIDTH

SC_CONV = 3

V7X_LANES = 128
V7X_SUBLANES = 8
V7X_MXU_DIM = 256
V7X_VMEM_LIMIT_BYTES = 56 * 1024 * 1024

ROW_TILE = 512
FFN_ROW_TILE = 1024
GROUP_ROWS = 256
FF_CHUNKS = (6 * V7X_MXU_DIM, 5 * V7X_MXU_DIM)
assert sum(FF_CHUNKS) == D_FF
HEADS_PER_LANE_GROUP = V7X_LANES // HEAD_DIM
MOBA_LANE_GROUPS = 4
MOBA_Q_BLOCKS = 2
MOBA_SCORE_LOOKAHEAD = 2
LOG2_E = 1.4426950408889634


def _rms(x, g):
    return x * lax.rsqrt(jnp.mean(x * x, axis=-1, keepdims=True) + NORM_EPS) * g


def _dot(a, b):
    return jnp.dot(a, b, preferred_element_type=F32)


def _resident(shape, index_map):
    return pl.BlockSpec(shape, index_map, pipeline_mode=pl.Buffered(1))


def _half_ffn_step(x, gpre_ref, gpost_ref, wg_ref, wu_ref, wd_ref):
    h = _rms(x, gpre_ref[...]).astype(BF16)
    acc = None
    for c in range(len(FF_CHUNKS)):
        sl = slice(sum(FF_CHUNKS[:c]), sum(FF_CHUNKS[:c + 1]))
        g = _dot(h, wg_ref[:, sl])
        u = _dot(h, wu_ref[:, sl])
        a = (g * jax.nn.sigmoid(g) * u).astype(BF16)
        d = _dot(a, wd_ref[sl, :])
        acc = d if acc is None else acc + d
    return x + 0.5 * _rms(acc, gpost_ref[...])


def _row_groups_of(ref):
    return [slice(r, r + GROUP_ROWS) for r in range(0, ref.shape[0], GROUP_ROWS)]


def _ffn_body(x_ref, gpre_ref, gpost_ref, wg_ref, wu_ref, wd_ref, o_ref):
    for rs in _row_groups_of(x_ref):
        o_ref[rs, :] = _half_ffn_step(x_ref[rs, :], gpre_ref, gpost_ref, wg_ref, wu_ref, wd_ref)


def _mix_ffn_body(x_ref, attn_ref, rec_ref, gmix_ref, wo_ref, gpre_ref, gpost_ref, wg_ref, wu_ref, wd_ref, o_ref):
    xs = []
    for rs in _row_groups_of(x_ref):
        mix = _dot(attn_ref[rs, :], wo_ref[:ATTN_WIDTH, :]) + _dot(rec_ref[rs, :], wo_ref[ATTN_WIDTH:, :])
        xs.append(x_ref[rs, :] + _rms(mix, gmix_ref[...]))
    for rs, x in zip(_row_groups_of(x_ref), xs):
        o_ref[rs, :] = _half_ffn_step(x, gpre_ref, gpost_ref, wg_ref, wu_ref, wd_ref)


def _ffn(x, gpre, gpost, wg, wu, wd, layer, mixer=None):
    m, d = x.shape
    tile = ROW_TILE if mixer is not None else FFN_ROW_TILE
    row_spec = pl.BlockSpec((tile, d), lambda i: (i, 0))
    vec_spec = _resident((1, d), lambda i: (0, 0))
    ffn_specs = [
        vec_spec, vec_spec,
        _resident((None, d, D_FF), lambda i: (layer, 0, 0)),
        _resident((None, d, D_FF), lambda i: (layer, 0, 0)),
        _resident((None, D_FF, d), lambda i: (layer, 0, 0)),
    ]
    if mixer is None:
        body, name = _ffn_body, "ffn"
        in_specs = [row_spec] + ffn_specs
        args = (x, gpre, gpost, wg, wu, wd)
    else:
        attn, rec, gmix, w_out, mixer_layer = mixer
        body, name = _mix_ffn_body, "mix_ffn"
        half_spec = pl.BlockSpec((tile, ATTN_WIDTH), lambda i: (i, 0))
        in_specs = [row_spec, half_spec, half_spec, vec_spec,
                    _resident((None, ATTN_WIDTH + LRU_WIDTH, d), lambda i: (mixer_layer, 0, 0))] + ffn_specs
        args = (x, attn, rec, gmix, w_out, gpre, gpost, wg, wu, wd)
    return pl.pallas_call(
        body,
        grid=(m // tile,),
        in_specs=in_specs,
        out_specs=row_spec,
        out_shape=jax.ShapeDtypeStruct((m, d), F32),
        compiler_params=pltpu.CompilerParams(
            dimension_semantics=("parallel",), vmem_limit_bytes=V7X_VMEM_LIMIT_BYTES),
        name=name,
    )(*args)


def _sconv_body(x_ref, gpre_ref, gpost_ref, win_ref, cw_ref, wout_ref, o_ref, tail_ref):
    @pl.when(pl.program_id(1) == 0)
    def _():
        tail_ref[...] = jnp.zeros_like(tail_ref)

    d = x_ref.shape[-1]
    tail = tail_ref[...]
    groups = _row_groups_of(x_ref)
    gated = []
    for rs in groups:
        h = _rms(x_ref[rs, :], gpre_ref[...]).astype(BF16)
        p_conv = _dot(h, win_ref[:, d:])
        gate_b = _dot(h, win_ref[:, :d])
        cx = p_conv[:, :d] * p_conv[:, d:]
        cx3 = _row_groups(cx)
        conv = cw_ref[SC_CONV - 1:SC_CONV, :] * cx
        for k in range(1, SC_CONV):
            conv = conv + cw_ref[SC_CONV - 1 - k:SC_CONV - k, :] * _shift_rows_grouped(cx3, tail, k).reshape(cx.shape)
        tail = cx3[-1:]
        gated.append((gate_b * conv).astype(BF16))
    for rs, y in zip(groups, gated):
        o_ref[rs, :] = x_ref[rs, :] + _rms(_dot(y, wout_ref[...]), gpost_ref[...])
    tail_ref[...] = tail


def _sconv(x, gpre, gpost, w_in, conv_w, w_out, layer, batch, seq):
    m, d = x.shape
    n_s = seq // ROW_TILE
    row_spec = pl.BlockSpec((ROW_TILE, d), lambda b, s: (b * n_s + s, 0))
    vec_spec = _resident((1, d), lambda b, s: (0, 0))
    return pl.pallas_call(
        _sconv_body,
        grid=(batch, n_s),
        in_specs=[
            row_spec, vec_spec, vec_spec,
            _resident((None, d, 3 * d), lambda b, s: (layer, 0, 0)),
            _resident((None, SC_CONV, d), lambda b, s: (layer, 0, 0)),
            _resident((None, d, d), lambda b, s: (layer, 0, 0)),
        ],
        out_specs=row_spec,
        out_shape=jax.ShapeDtypeStruct((m, d), F32),
        scratch_shapes=[pltpu.VMEM((1, V7X_SUBLANES, d), F32)],
        compiler_params=pltpu.CompilerParams(
            dimension_semantics=("arbitrary", "arbitrary"), vmem_limit_bytes=V7X_VMEM_LIMIT_BYTES),
        name="sconv",
    )(x, gpre, gpost, w_in, conv_w, w_out)


def _rope(z, cos, sin_up, sin_dn):
    half = ROT_DIM // 2
    outs = []
    for c in range(ATTN_WIDTH // V7X_LANES):
        zc = z[:, c * V7X_LANES:(c + 1) * V7X_LANES]
        up = pltpu.roll(zc, V7X_LANES - half, axis=1)
        dn = pltpu.roll(zc, half, axis=1)
        outs.append(zc * cos + up * sin_up + dn * sin_dn)
    return jnp.concatenate(outs, axis=1)


def _row_groups(z):
    return z.reshape(z.shape[0] // V7X_SUBLANES, V7X_SUBLANES, z.shape[1])


def _shift_rows_grouped(cur, tail, k):
    sub = lax.broadcasted_iota(jnp.int32, (1, V7X_SUBLANES, 1), 1)
    rot = pltpu.roll(cur, k, axis=1)
    prev = jnp.concatenate([pltpu.roll(tail, k, axis=1), rot[:-1]], axis=0)
    return jnp.where(sub >= k, rot, prev)


def _scan_rows(a, u, h_in):
    sub = lax.broadcasted_iota(jnp.int32, (1, V7X_SUBLANES, 1), 1)
    d = 1
    while d < V7X_SUBLANES:
        keep = sub >= d
        a_s = jnp.where(keep, pltpu.roll(a, d, axis=1), 1.0)
        u_s = jnp.where(keep, pltpu.roll(u, d, axis=1), 0.0)
        u = a * u_s + u
        a = a * a_s
        d *= 2
    h = h_in
    out = []
    for g in range(a.shape[0]):
        hg = u[g] + a[g] * h
        out.append(hg)
        h = hg[V7X_SUBLANES - 1:, :]
    return jnp.concatenate(out, axis=0), h


def _proj_lru_body(x_ref, gpre_ref, win_ref, cos_ref, sup_ref, sdn_ref, cw_ref, cb_ref,
                   wa_ref, wx_ref, ba_ref, bx_ref, lam_ref,
                   q_ref, k_ref, v_ref, km_ref, rec_ref, xtail_ref, hprev_ref):
    @pl.when(pl.program_id(1) == 0)
    def _():
        xtail_ref[...] = jnp.zeros_like(xtail_ref)
        hprev_ref[...] = jnp.zeros_like(hprev_ref)

    w = ATTN_WIDTH
    xtail = xtail_ref[...]
    h_state = hprev_ref[...]
    softplus_neg_lam = jax.nn.softplus(-lam_ref[...])
    for g in range(ROW_TILE // MOBA_BLOCK):
        rs = slice(g * MOBA_BLOCK, (g + 1) * MOBA_BLOCK)
        h = _rms(x_ref[rs, :], gpre_ref[...]).astype(BF16)
        p_lru = _dot(h, win_ref[:, 3 * w:])
        p = _dot(h, win_ref[:, :3 * w])

        cos, sup, sdn = cos_ref[rs, :], sup_ref[rs, :], sdn_ref[rs, :]
        q_ref[rs, :] = (_rope(p[:, :w], cos, sup, sdn) * (HEAD_DIM ** -0.5 * LOG2_E)).astype(BF16)
        k = _rope(p[:, w:2 * w], cos, sup, sdn)
        k_ref[rs, :] = k.astype(BF16)
        v_ref[rs, :] = p[:, 2 * w:3 * w].astype(BF16)
        km_ref[g] = jnp.mean(k, axis=0, keepdims=True)

        lx = p_lru[:, :LRU_WIDTH]
        lg = p_lru[:, LRU_WIDTH:]
        lx3 = _row_groups(lx)
        xc = cw_ref[LRU_CONV - 1:LRU_CONV, :] * lx + cb_ref[...]
        for kk in range(1, LRU_CONV):
            shifted = _shift_rows_grouped(lx3, xtail, kk).reshape(lx.shape)
            xc = xc + cw_ref[LRU_CONV - 1 - kk:LRU_CONV - kk, :] * shifted
        xtail = lx3[-1:]

        xcb = xc.astype(BF16)
        r = jax.nn.sigmoid(_dot(xcb, wa_ref[...]) + ba_ref[...])
        i = jax.nn.sigmoid(_dot(xcb, wx_ref[...]) + bx_ref[...])
        a = jnp.exp((-LRU_C) * r * softplus_neg_lam)
        z = 1.0 - a * a
        root = jnp.where(z > 0.0, z * lax.rsqrt(z), 0.0)
        u = root * (i * xc)
        hs, h_state = _scan_rows(_row_groups(a), _row_groups(u), h_state)
        rec_ref[rs, :] = (hs * jax.nn.gelu(lg)).astype(BF16)
    xtail_ref[...] = xtail
    hprev_ref[...] = h_state


def _proj_lru(x, gpre, w_in, cos, sup, sdn, conv_w, conv_b, wa, wx, ba, bx, lam, layer, batch, seq):
    m, d = x.shape
    n_s = seq // ROW_TILE
    blocks_per_tile = ROW_TILE // MOBA_BLOCK

    def rows(width):
        return pl.BlockSpec((ROW_TILE, width), lambda b, s: (b * n_s + s, 0))

    def res(shape):
        zeros = (0,) * (len(shape) - 1)
        return _resident(shape, lambda b, s: (layer,) + zeros)

    tab_spec = pl.BlockSpec((ROW_TILE, V7X_LANES), lambda b, s: (s, 0))
    return pl.pallas_call(
        _proj_lru_body,
        grid=(batch, n_s),
        in_specs=[
            rows(d), _resident((1, d), lambda b, s: (0, 0)),
            res((None, d, AB_IN)),
            tab_spec, tab_spec, tab_spec,
            res((None, LRU_CONV, LRU_WIDTH)), res((None, 1, LRU_WIDTH)),
            res((None, LRU_WIDTH, LRU_WIDTH)), res((None, LRU_WIDTH, LRU_WIDTH)),
            res((None, 1, LRU_WIDTH)), res((None, 1, LRU_WIDTH)), res((None, 1, LRU_WIDTH)),
        ],
        out_specs=[
            rows(ATTN_WIDTH), rows(ATTN_WIDTH), rows(ATTN_WIDTH),
            pl.BlockSpec((blocks_per_tile, 1, ATTN_WIDTH), lambda b, s: (b * n_s + s, 0, 0)),
            rows(LRU_WIDTH),
        ],
        out_shape=[
            jax.ShapeDtypeStruct((m, ATTN_WIDTH), BF16),
            jax.ShapeDtypeStruct((m, ATTN_WIDTH), BF16),
            jax.ShapeDtypeStruct((m, ATTN_WIDTH), BF16),
            jax.ShapeDtypeStruct((m // MOBA_BLOCK, 1, ATTN_WIDTH), F32),
            jax.ShapeDtypeStruct((m, LRU_WIDTH), BF16),
        ],
        scratch_shapes=[pltpu.VMEM((1, V7X_SUBLANES, LRU_WIDTH), F32),
                        pltpu.VMEM((1, LRU_WIDTH), F32)],
        compiler_params=pltpu.CompilerParams(
            dimension_semantics=("arbitrary", "arbitrary"), vmem_limit_bytes=V7X_VMEM_LIMIT_BYTES),
        name="proj_lru",
    )(x, gpre, w_in, cos, sup, sdn, conv_w, conv_b, wa, wx, ba, bx, lam)


_NT = (((1,), (1,)), ((), ()))
_TN = (((0,), (0,)), ((), ()))


def _block_gate_bias(q2, km, qi, n_blk):
    nq = q2.shape[0]
    km_hi = km.astype(BF16).astype(F32)
    km_mid = (km - km_hi).astype(BF16).astype(F32)
    km_lo = (km - km_hi - km_mid).astype(BF16).astype(F32)
    km3 = jnp.concatenate([km_hi, km_mid, km_lo], axis=0).astype(BF16)
    g3 = lax.dot_general(km3, q2, _NT, preferred_element_type=F32)
    g = g3[:n_blk] + g3[n_blk:2 * n_blk] + g3[2 * n_blk:]
    bid = lax.broadcasted_iota(jnp.int32, (n_blk, nq), 0)
    g = jnp.where(bid < qi, g, MASK_VALUE)
    bias = jnp.full((n_blk, nq), MASK_VALUE, F32)
    for j in range(qi):
        gj = g[j:j + 1, :]
        beats = (g > gj) | ((g == gj) & (bid < j))
        rank = jnp.sum(beats.astype(jnp.int32), axis=0, keepdims=True)
        bias = jnp.where((bid == j) & (rank < MOBA_TOPK), 0.0, bias)
    return bias


def _moba_body(q_ref, k_ref, v_ref, km_ref, o_ref, *, n_blk):
    step = pl.program_id(2)
    blk = MOBA_BLOCK
    nq = HEADS_PER_LANE_GROUP * blk
    lane = lax.broadcasted_iota(jnp.int32, (1, V7X_LANES), 1)
    kpos = lax.broadcasted_iota(jnp.int32, (blk, nq), 0)
    qpos = lax.broadcasted_iota(jnp.int32, (blk, nq), 1) % blk
    causal = kpos <= qpos

    def lanes(c):
        return slice(c * V7X_LANES, (c + 1) * V7X_LANES)

    def scores(qi, t, c):
        q = q_ref[0, t * blk:(t + 1) * blk, lanes(c)]
        zero = jnp.zeros_like(q)
        q2 = jnp.concatenate(
            [jnp.where((lane // HEAD_DIM) == hh, q, zero) for hh in range(HEADS_PER_LANE_GROUP)], axis=0)
        bias = _block_gate_bias(q2, km_ref[0, :, lanes(c)], qi, n_blk)
        return lax.dot_general(k_ref[0, :(qi + 1) * blk, lanes(c)], q2, _NT, preferred_element_type=F32), bias

    def probs(qi, s, bias):
        own = jnp.where(causal, s[qi * blk:], MASK_VALUE)
        tops = [jnp.max(s[j * blk:(j + 1) * blk], axis=0, keepdims=True) + bias[j:j + 1, :] for j in range(qi)]
        tops.append(jnp.max(own, axis=0, keepdims=True))
        m = functools.reduce(jnp.maximum, tops)
        parts = [jnp.exp2(s[j * blk:(j + 1) * blk] - (m - bias[j:j + 1, :])) for j in range(qi)]
        parts.append(jnp.exp2(own - m))
        p = jnp.concatenate(parts, axis=0)
        return p.astype(BF16), jnp.sum(p, axis=0, keepdims=True)

    def values(qi, t, c, p, l):
        acc = lax.dot_general(v_ref[0, :(qi + 1) * blk, lanes(c)], p, _TN, preferred_element_type=F32)
        out_t = acc / l
        out_t = jnp.concatenate(
            [out_t[hh * HEAD_DIM:(hh + 1) * HEAD_DIM, hh * blk:(hh + 1) * blk]
             for hh in range(HEADS_PER_LANE_GROUP)], axis=0)
        o_ref[0, t * blk:(t + 1) * blk, lanes(c)] = out_t.T.astype(BF16)

    def variant(i):
        chains = [(MOBA_Q_BLOCKS * i + t, t, c) for t in range(MOBA_Q_BLOCKS) for c in range(MOBA_LANE_GROUPS)]
        pending = [scores(*ch) for ch in chains[:MOBA_SCORE_LOOKAHEAD]]
        for idx, (qi, t, c) in enumerate(chains):
            if idx + MOBA_SCORE_LOOKAHEAD < len(chains):
                pending.append(scores(*chains[idx + MOBA_SCORE_LOOKAHEAD]))
            values(qi, t, c, *probs(qi, *pending[idx]))

    for i in range(n_blk // MOBA_Q_BLOCKS):
        pl.when(step == i)(functools.partial(variant, i))


def _moba(q, k, v, kmean, batch, seq):
    n_blk = seq // MOBA_BLOCK
    width = MOBA_LANE_GROUPS * V7X_LANES
    q3 = q.reshape(batch, seq, ATTN_WIDTH)
    k3 = k.reshape(batch, seq, ATTN_WIDTH)
    v3 = v.reshape(batch, seq, ATTN_WIDTH)
    km3 = kmean.reshape(batch, n_blk, ATTN_WIDTH)
    q_spec = pl.BlockSpec((1, MOBA_Q_BLOCKS * MOBA_BLOCK, width), lambda b, hg, i: (b, i, hg))
    kv_spec = pl.BlockSpec((1, seq, width), lambda b, hg, i: (b, 0, hg))
    out = pl.pallas_call(
        functools.partial(_moba_body, n_blk=n_blk),
        grid=(batch, ATTN_WIDTH // width, n_blk // MOBA_Q_BLOCKS),
        in_specs=[q_spec, kv_spec, kv_spec,
                  pl.BlockSpec((1, n_blk, width), lambda b, hg, i: (b, 0, hg))],
        out_specs=q_spec,
        out_shape=jax.ShapeDtypeStruct((batch, seq, ATTN_WIDTH), BF16),
        compiler_params=pltpu.CompilerParams(
            dimension_semantics=("parallel", "parallel", "arbitrary"),
            vmem_limit_bytes=V7X_VMEM_LIMIT_BYTES),
        name="moba",
    )(q3, k3, v3, km3)
    return out.reshape(batch * seq, ATTN_WIDTH)


def _rope_tables(seq):
    half = ROT_DIM // 2
    pos = jnp.arange(seq, dtype=F32)
    inv_freq = ROPE_THETA ** (-jnp.arange(0, ROT_DIM, 2, dtype=F32) / ROT_DIM)
    ang = pos[:, None] * inv_freq[None, :]
    cos, sin = jnp.cos(ang), jnp.sin(ang)
    ones = jnp.ones((seq, HEAD_DIM - ROT_DIM), F32)
    zeros = jnp.zeros((seq, HEAD_DIM - ROT_DIM), F32)
    zhalf = jnp.zeros((seq, half), F32)
    cos_h = jnp.concatenate([cos, cos, ones], axis=1)
    sup_h = jnp.concatenate([-sin, zhalf, zeros], axis=1)
    sdn_h = jnp.concatenate([zhalf, sin, zeros], axis=1)
    rep = V7X_LANES // HEAD_DIM
    return jnp.tile(cos_h, (1, rep)), jnp.tile(sup_h, (1, rep)), jnp.tile(sdn_h, (1, rep))


def _block_diag(w):
    n, h, d, _ = w.shape
    eye = jnp.eye(h, dtype=w.dtype)
    return jnp.einsum("nhij,hg->nhigj", w, eye).reshape(n, h * d, h * d)


def kernel(x, ffn1_w_gate, ffn1_w_up, ffn1_w_down, ffn2_w_gate, ffn2_w_up, ffn2_w_down, norm_pre, norm_post,
           ab_w_in, ab_w_out, lru_conv_w, lru_conv_b, lru_gate_a_w, lru_gate_a_b, lru_gate_x_w, lru_gate_x_b,
           lru_lambda, c_w_in, c_conv_w, c_w_out):
    batch, seq, d = x.shape
    depth = norm_pre.shape[0]
    assert d == D_MODEL and seq % ROW_TILE == 0 and ROW_TILE % MOBA_BLOCK == 0
    assert (batch * seq) % FFN_ROW_TILE == 0 and FFN_ROW_TILE % GROUP_ROWS == 0 and ROW_TILE % GROUP_ROWS == 0
    assert seq // MOBA_BLOCK > MOBA_TOPK and (seq // MOBA_BLOCK) % MOBA_Q_BLOCKS == 0

    ffn1 = [w.astype(BF16) for w in (ffn1_w_gate, ffn1_w_up, ffn1_w_down)]
    ffn2 = [w.astype(BF16) for w in (ffn2_w_gate, ffn2_w_up, ffn2_w_down)]
    ab_in, ab_out = ab_w_in.astype(BF16), ab_w_out.astype(BF16)
    c_in, c_out = c_w_in.astype(BF16), c_w_out.astype(BF16)
    wa = _block_diag(lru_gate_a_w).astype(BF16)
    wx = _block_diag(lru_gate_x_w).astype(BF16)
    n_even = ab_w_in.shape[0]
    ba = lru_gate_a_b.reshape(n_even, 1, LRU_WIDTH)
    bx = lru_gate_x_b.reshape(n_even, 1, LRU_WIDTH)
    lam = lru_lambda.reshape(n_even, 1, LRU_WIDTH)
    conv_b = lru_conv_b.reshape(n_even, 1, LRU_WIDTH)
    cos, sup, sdn = _rope_tables(seq)

    xs = x.reshape(batch * seq, d)
    for l in range(depth):
        pre = lambda j: norm_pre[l, j].reshape(1, d)
        post = lambda j: norm_post[l, j].reshape(1, d)
        xs = _ffn(xs, pre(0), post(0), *ffn1, l)
        mixer = None
        if l % 2 == 0:
            e = l // 2
            q, k, v, kmean, rec = _proj_lru(xs, pre(1), ab_in, cos, sup, sdn, lru_conv_w, conv_b,
                                            wa, wx, ba, bx, lam, e, batch, seq)
            attn = _moba(q, k, v, kmean, batch, seq)
            mixer = (attn, rec, post(1), ab_out, e)
        else:
            xs = _sconv(xs, pre(1), post(1), c_in, c_conv_w, c_out, l // 2, batch, seq)
        xs = _ffn(xs, pre(2), post(2), *ffn2, l, mixer=mixer)
    return xs.reshape(batch, seq, d)
```

```python
import functools

import jax
import jax.numpy as jnp
from jax import lax
from jax.experimental import pallas as pl
from jax.experimental.pallas import tpu as pltpu

F32 = jnp.float32
BF16 = jnp.bfloat16

NORM_EPS = 1e-6
D_MODEL = 1024
D_FF = 2816

ATTN_HEADS = 8
HEAD_DIM = 64
ATTN_WIDTH = ATTN_HEADS * HEAD_DIM
ROT_DIM = HEAD_DIM // 4
ROPE_THETA = 500000.0
MOBA_BLOCK = 256
MOBA_TOPK = 3
MASK_VALUE = -1e30

LRU_WIDTH = 512
LRU_HEADS = 8
LRU_CONV = 4
LRU_C = 8.0
AB_IN = 3 * ATTN_WIDTH + 2 * LRU_WIDTH

SC_CONV = 3

V7X_LANES = 128
V7X_SUBLANES = 8
V7X_MXU_DIM = 256
V7X_VMEM_LIMIT_BYTES = 56 * 1024 * 1024

ROW_TILE = 512
FFN_ROW_TILE = 1024
GROUP_ROWS = 256
W_STAGE_CHUNKS = 8
FF_CHUNKS = (6 * V7X_MXU_DIM, 5 * V7X_MXU_DIM)
assert sum(FF_CHUNKS) == D_FF
HEADS_PER_LANE_GROUP = V7X_LANES // HEAD_DIM
MOBA_LANE_GROUPS = 4
MOBA_Q_BLOCKS = 2
MOBA_SCORE_LOOKAHEAD = 2
LOG2_E = 1.4426950408889634


def _rms(x, g):
    return x * lax.rsqrt(jnp.mean(x * x, axis=-1, keepdims=True) + NORM_EPS) * g


def _dot(a, b):
    return jnp.dot(a, b, preferred_element_type=F32)


def _resident(shape, index_map):
    return pl.BlockSpec(shape, index_map, pipeline_mode=pl.Buffered(1))


def _half_ffn_step(x, gpre_ref, gpost_ref, wg_ref, wu_ref, wd_ref):
    h = _rms(x, gpre_ref[...]).astype(BF16)
    acc = None
    for c in range(len(FF_CHUNKS)):
        sl = slice(sum(FF_CHUNKS[:c]), sum(FF_CHUNKS[:c + 1]))
        g = _dot(h, wg_ref[:, sl])
        u = _dot(h, wu_ref[:, sl])
        a = (g * jax.nn.sigmoid(g) * u).astype(BF16)
        d = _dot(a, wd_ref[sl, :])
        acc = d if acc is None else acc + d
    return x + 0.5 * _rms(acc, gpost_ref[...])


def _row_groups_of(ref):
    return [slice(r, r + GROUP_ROWS) for r in range(0, ref.shape[0], GROUP_ROWS)]


def _weight_chunk_copies(layer, stage_groups, sem):
    jobs = []
    for grp, (stage, weights) in enumerate(stage_groups):
        rows = stage.shape[1]
        k = 0
        for hbm, dst in weights:
            for c in range(dst.shape[0] // rows):
                slot = k % 2
                src = hbm.at[layer, pl.ds(c * rows, rows), :]
                copy = pltpu.make_async_copy(src, stage.at[slot], sem.at[2 * grp + slot])
                jobs.append((copy, stage.at[slot], dst.at[pl.ds(c * rows, rows), :]))
                k += 1
    return jobs


def _stage_weights_bf16(layer, stage_groups, sem):
    @pl.when(pl.program_id(0) == 0)
    def _():
        jobs = _weight_chunk_copies(layer, stage_groups, sem)
        jobs[0][0].start()
        for j, (copy, staged, dst) in enumerate(jobs):
            if j + 1 < len(jobs):
                jobs[j + 1][0].start()
            copy.wait()
            dst[...] = staged[...].astype(BF16)


def _ffn_body(layer, x_ref, gpre_ref, gpost_ref, wg_hbm, wu_hbm, wd_hbm, o_ref,
              wg_ref, wu_ref, wd_ref, stage_in, stage_down, sem):
    _stage_weights_bf16(layer, [(stage_in, [(wg_hbm, wg_ref), (wu_hbm, wu_ref)]),
                                (stage_down, [(wd_hbm, wd_ref)])], sem)
    for rs in _row_groups_of(x_ref):
        o_ref[rs, :] = _half_ffn_step(x_ref[rs, :], gpre_ref, gpost_ref, wg_ref, wu_ref, wd_ref)


def _mix_ffn_body(layer, x_ref, attn_ref, rec_ref, gmix_ref, wo_ref, gpre_ref, gpost_ref, wg_hbm, wu_hbm, wd_hbm,
                  o_ref, wg_ref, wu_ref, wd_ref, stage_in, stage_down, sem):
    _stage_weights_bf16(layer, [(stage_in, [(wg_hbm, wg_ref), (wu_hbm, wu_ref)]),
                                (stage_down, [(wd_hbm, wd_ref)])], sem)

    xs = []
    for rs in _row_groups_of(x_ref):
        mix = _dot(attn_ref[rs, :], wo_ref[:ATTN_WIDTH, :]) + _dot(rec_ref[rs, :], wo_ref[ATTN_WIDTH:, :])
        xs.append(x_ref[rs, :] + _rms(mix, gmix_ref[...]))
    for rs, x in zip(_row_groups_of(x_ref), xs):
        o_ref[rs, :] = _half_ffn_step(x, gpre_ref, gpost_ref, wg_ref, wu_ref, wd_ref)


def _ffn(x, gpre, gpost, wg, wu, wd, layer, mixer=None):
    m, d = x.shape
    tile = ROW_TILE if mixer is not None else FFN_ROW_TILE
    row_spec = pl.BlockSpec((tile, d), lambda i: (i, 0))
    vec_spec = _resident((1, d), lambda i: (0, 0))
    hbm_spec = pl.BlockSpec(memory_space=pl.ANY)
    ffn_specs = [vec_spec, vec_spec, hbm_spec, hbm_spec, hbm_spec]
    if mixer is None:
        body, name = _ffn_body, "ffn"
        in_specs = [row_spec] + ffn_specs
        args = (x, gpre, gpost, wg, wu, wd)
    else:
        attn, rec, gmix, w_out, mixer_layer = mixer
        body, name = _mix_ffn_body, "mix_ffn"
        half_spec = pl.BlockSpec((tile, ATTN_WIDTH), lambda i: (i, 0))
        in_specs = [row_spec, half_spec, half_spec, vec_spec,
                    _resident((None, ATTN_WIDTH + LRU_WIDTH, d), lambda i: (mixer_layer, 0, 0))] + ffn_specs
        args = (x, attn, rec, gmix, w_out, gpre, gpost, wg, wu, wd)
    return pl.pallas_call(
        functools.partial(body, layer),
        grid=(m // tile,),
        in_specs=in_specs,
        out_specs=row_spec,
        out_shape=jax.ShapeDtypeStruct((m, d), F32),
        scratch_shapes=[
            pltpu.VMEM((d, D_FF), BF16), pltpu.VMEM((d, D_FF), BF16), pltpu.VMEM((D_FF, d), BF16),
            pltpu.VMEM((2, d // W_STAGE_CHUNKS, D_FF), F32),
            pltpu.VMEM((2, D_FF // W_STAGE_CHUNKS, d), F32),
            pltpu.SemaphoreType.DMA((4,)),
        ],
        compiler_params=pltpu.CompilerParams(
            dimension_semantics=("arbitrary",), vmem_limit_bytes=V7X_VMEM_LIMIT_BYTES),
        name=name,
    )(*args)


def _sconv_body(x_ref, gpre_ref, gpost_ref, win_ref, cw_ref, wout_ref, o_ref, tail_ref):
    @pl.when(pl.program_id(1) == 0)
    def _():
        tail_ref[...] = jnp.zeros_like(tail_ref)

    d = x_ref.shape[-1]
    tail = tail_ref[...]
    groups = _row_groups_of(x_ref)
    gated = []
    for rs in groups:
        h = _rms(x_ref[rs, :], gpre_ref[...]).astype(BF16)
        p_conv = _dot(h, win_ref[:, d:])
        gate_b = _dot(h, win_ref[:, :d])
        cx = p_conv[:, :d] * p_conv[:, d:]
        cx3 = _row_groups(cx)
        conv = cw_ref[SC_CONV - 1:SC_CONV, :] * cx
        for k in range(1, SC_CONV):
            conv = conv + cw_ref[SC_CONV - 1 - k:SC_CONV - k, :] * _shift_rows_grouped(cx3, tail, k).reshape(cx.shape)
        tail = cx3[-1:]
        gated.append((gate_b * conv).astype(BF16))
    for rs, y in zip(groups, gated):
        o_ref[rs, :] = x_ref[rs, :] + _rms(_dot(y, wout_ref[...]), gpost_ref[...])
    tail_ref[...] = tail


def _sconv(x, gpre, gpost, w_in, conv_w, w_out, layer, batch, seq):
    m, d = x.shape
    n_s = seq // ROW_TILE
    row_spec = pl.BlockSpec((ROW_TILE, d), lambda b, s: (b * n_s + s, 0))
    vec_spec = _resident((1, d), lambda b, s: (0, 0))
    return pl.pallas_call(
        _sconv_body,
        grid=(batch, n_s),
        in_specs=[
            row_spec, vec_spec, vec_spec,
            _resident((None, d, 3 * d), lambda b, s: (layer, 0, 0)),
            _resident((None, SC_CONV, d), lambda b, s: (layer, 0, 0)),
            _resident((None, d, d), lambda b, s: (layer, 0, 0)),
        ],
        out_specs=row_spec,
        out_shape=jax.ShapeDtypeStruct((m, d), F32),
        scratch_shapes=[pltpu.VMEM((1, V7X_SUBLANES, d), F32)],
        compiler_params=pltpu.CompilerParams(
            dimension_semantics=("arbitrary", "arbitrary"), vmem_limit_bytes=V7X_VMEM_LIMIT_BYTES),
        name="sconv",
    )(x, gpre, gpost, w_in, conv_w, w_out)


def _rope(z, cos, sin_up, sin_dn):
    half = ROT_DIM // 2
    outs = []
    for c in range(ATTN_WIDTH // V7X_LANES):
        zc = z[:, c * V7X_LANES:(c + 1) * V7X_LANES]
        up = pltpu.roll(zc, V7X_LANES - half, axis=1)
        dn = pltpu.roll(zc, half, axis=1)
        outs.append(zc * cos + up * sin_up + dn * sin_dn)
    return jnp.concatenate(outs, axis=1)


def _row_groups(z):
    return z.reshape(z.shape[0] // V7X_SUBLANES, V7X_SUBLANES, z.shape[1])


def _shift_rows_grouped(cur, tail, k):
    sub = lax.broadcasted_iota(jnp.int32, (1, V7X_SUBLANES, 1), 1)
    rot = pltpu.roll(cur, k, axis=1)
    prev = jnp.concatenate([pltpu.roll(tail, k, axis=1), rot[:-1]], axis=0)
    return jnp.where(sub >= k, rot, prev)


def _scan_rows(a, u, h_in):
    sub = lax.broadcasted_iota(jnp.int32, (1, V7X_SUBLANES, 1), 1)
    d = 1
    while d < V7X_SUBLANES:
        keep = sub >= d
        a_s = jnp.where(keep, pltpu.roll(a, d, axis=1), 1.0)
        u_s = jnp.where(keep, pltpu.roll(u, d, axis=1), 0.0)
        u = a * u_s + u
        a = a * a_s
        d *= 2
    h = h_in
    out = []
    for g in range(a.shape[0]):
        hg = u[g] + a[g] * h
        out.append(hg)
        h = hg[V7X_SUBLANES - 1:, :]
    return jnp.concatenate(out, axis=0), h


def _proj_lru_body(x_ref, gpre_ref, win_ref, cos_ref, sup_ref, sdn_ref, cw_ref, cb_ref,
                   wa_ref, wx_ref, ba_ref, bx_ref, lam_ref,
                   q_ref, k_ref, v_ref, km_ref, rec_ref, xtail_ref, hprev_ref):
    @pl.when(pl.program_id(1) == 0)
    def _():
        xtail_ref[...] = jnp.zeros_like(xtail_ref)
        hprev_ref[...] = jnp.zeros_like(hprev_ref)

    w = ATTN_WIDTH
    xtail = xtail_ref[...]
    h_state = hprev_ref[...]
    softplus_neg_lam = jax.nn.softplus(-lam_ref[...])
    for g in range(ROW_TILE // MOBA_BLOCK):
        rs = slice(g * MOBA_BLOCK, (g + 1) * MOBA_BLOCK)
        h = _rms(x_ref[rs, :], gpre_ref[...]).astype(BF16)
        p_lru = _dot(h, win_ref[:, 3 * w:])
        p = _dot(h, win_ref[:, :3 * w])

        cos, sup, sdn = cos_ref[rs, :], sup_ref[rs, :], sdn_ref[rs, :]
        q_ref[rs, :] = (_rope(p[:, :w], cos, sup, sdn) * (HEAD_DIM ** -0.5 * LOG2_E)).astype(BF16)
        k = _rope(p[:, w:2 * w], cos, sup, sdn)
        k_ref[rs, :] = k.astype(BF16)
        v_ref[rs, :] = p[:, 2 * w:3 * w].astype(BF16)
        km_ref[g] = jnp.mean(k, axis=0, keepdims=True)

        lx = p_lru[:, :LRU_WIDTH]
        lg = p_lru[:, LRU_WIDTH:]
        lx3 = _row_groups(lx)
        xc = cw_ref[LRU_CONV - 1:LRU_CONV, :] * lx + cb_ref[...]
        for kk in range(1, LRU_CONV):
            shifted = _shift_rows_grouped(lx3, xtail, kk).reshape(lx.shape)
            xc = xc + cw_ref[LRU_CONV - 1 - kk:LRU_CONV - kk, :] * shifted
        xtail = lx3[-1:]

        xcb = xc.astype(BF16)
        r = jax.nn.sigmoid(_dot(xcb, wa_ref[...]) + ba_ref[...])
        i = jax.nn.sigmoid(_dot(xcb, wx_ref[...]) + bx_ref[...])
        a = jnp.exp((-LRU_C) * r * softplus_neg_lam)
        z = 1.0 - a * a
        root = jnp.where(z > 0.0, z * lax.rsqrt(z), 0.0)
        u = root * (i * xc)
        hs, h_state = _scan_rows(_row_groups(a), _row_groups(u), h_state)
        rec_ref[rs, :] = (hs * jax.nn.gelu(lg)).astype(BF16)
    xtail_ref[...] = xtail
    hprev_ref[...] = h_state


def _proj_lru(x, gpre, w_in, cos, sup, sdn, conv_w, conv_b, wa, wx, ba, bx, lam, layer, batch, seq):
    m, d = x.shape
    n_s = seq // ROW_TILE
    blocks_per_tile = ROW_TILE // MOBA_BLOCK

    def rows(width):
        return pl.BlockSpec((ROW_TILE, width), lambda b, s: (b * n_s + s, 0))

    def res(shape):
        zeros = (0,) * (len(shape) - 1)
        return _resident(shape, lambda b, s: (layer,) + zeros)

    tab_spec = pl.BlockSpec((ROW_TILE, V7X_LANES), lambda b, s: (s, 0))
    return pl.pallas_call(
        _proj_lru_body,
        grid=(batch, n_s),
        in_specs=[
            rows(d), _resident((1, d), lambda b, s: (0, 0)),
            res((None, d, AB_IN)),
            tab_spec, tab_spec, tab_spec,
            res((None, LRU_CONV, LRU_WIDTH)), res((None, 1, LRU_WIDTH)),
            res((None, LRU_WIDTH, LRU_WIDTH)), res((None, LRU_WIDTH, LRU_WIDTH)),
            res((None, 1, LRU_WIDTH)), res((None, 1, LRU_WIDTH)), res((None, 1, LRU_WIDTH)),
        ],
        out_specs=[
            rows(ATTN_WIDTH), rows(ATTN_WIDTH), rows(ATTN_WIDTH),
            pl.BlockSpec((blocks_per_tile, 1, ATTN_WIDTH), lambda b, s: (b * n_s + s, 0, 0)),
            rows(LRU_WIDTH),
        ],
        out_shape=[
            jax.ShapeDtypeStruct((m, ATTN_WIDTH), BF16),
            jax.ShapeDtypeStruct((m, ATTN_WIDTH), BF16),
            jax.ShapeDtypeStruct((m, ATTN_WIDTH), BF16),
            jax.ShapeDtypeStruct((m // MOBA_BLOCK, 1, ATTN_WIDTH), F32),
            jax.ShapeDtypeStruct((m, LRU_WIDTH), BF16),
        ],
        scratch_shapes=[pltpu.VMEM((1, V7X_SUBLANES, LRU_WIDTH), F32),
                        pltpu.VMEM((1, LRU_WIDTH), F32)],
        compiler_params=pltpu.CompilerParams(
            dimension_semantics=("arbitrary", "arbitrary"), vmem_limit_bytes=V7X_VMEM_LIMIT_BYTES),
        name="proj_lru",
    )(x, gpre, w_in, cos, sup, sdn, conv_w, conv_b, wa, wx, ba, bx, lam)


_NT = (((1,), (1,)), ((), ()))
_TN = (((0,), (0,)), ((), ()))


def _block_gate_bias(q2, km, qi, n_blk):
    nq = q2.shape[0]
    km_hi = km.astype(BF16).astype(F32)
    km_mid = (km - km_hi).astype(BF16).astype(F32)
    km_lo = (km - km_hi - km_mid).astype(BF16).astype(F32)
    km3 = jnp.concatenate([km_hi, km_mid, km_lo], axis=0).astype(BF16)
    g3 = lax.dot_general(km3, q2, _NT, preferred_element_type=F32)
    g = g3[:n_blk] + g3[n_blk:2 * n_blk] + g3[2 * n_blk:]
    bid = lax.broadcasted_iota(jnp.int32, (n_blk, nq), 0)
    g = jnp.where(bid < qi, g, MASK_VALUE)
    bias = jnp.full((n_blk, nq), MASK_VALUE, F32)
    for j in range(qi):
        gj = g[j:j + 1, :]
        beats = (g > gj) | ((g == gj) & (bid < j))
        rank = jnp.sum(beats.astype(jnp.int32), axis=0, keepdims=True)
        bias = jnp.where((bid == j) & (rank < MOBA_TOPK), 0.0, bias)
    return bias


def _moba_body(q_ref, k_ref, v_ref, km_ref, o_ref, *, n_blk):
    step = pl.program_id(2)
    blk = MOBA_BLOCK
    nq = HEADS_PER_LANE_GROUP * blk
    lane = lax.broadcasted_iota(jnp.int32, (1, V7X_LANES), 1)
    kpos = lax.broadcasted_iota(jnp.int32, (blk, nq), 0)
    qpos = lax.broadcasted_iota(jnp.int32, (blk, nq), 1) % blk
    causal = kpos <= qpos

    def lanes(c):
        return slice(c * V7X_LANES, (c + 1) * V7X_LANES)

    def scores(qi, t, c):
        q = q_ref[0, t * blk:(t + 1) * blk, lanes(c)]
        zero = jnp.zeros_like(q)
        q2 = jnp.concatenate(
            [jnp.where((lane // HEAD_DIM) == hh, q, zero) for hh in range(HEADS_PER_LANE_GROUP)], axis=0)
        bias = _block_gate_bias(q2, km_ref[0, :, lanes(c)], qi, n_blk)
        return lax.dot_general(k_ref[0, :(qi + 1) * blk, lanes(c)], q2, _NT, preferred_element_type=F32), bias

    def probs(qi, s, bias):
        own = jnp.where(causal, s[qi * blk:], MASK_VALUE)
        tops = [jnp.max(s[j * blk:(j + 1) * blk], axis=0, keepdims=True) + bias[j:j + 1, :] for j in range(qi)]
        tops.append(jnp.max(own, axis=0, keepdims=True))
        m = functools.reduce(jnp.maximum, tops)
        parts = [jnp.exp2(s[j * blk:(j + 1) * blk] - (m - bias[j:j + 1, :])) for j in range(qi)]
        parts.append(jnp.exp2(own - m))
        p = jnp.concatenate(parts, axis=0)
        return p.astype(BF16), jnp.sum(p, axis=0, keepdims=True)

    def values(qi, t, c, p, l):
        acc = lax.dot_general(v_ref[0, :(qi + 1) * blk, lanes(c)], p, _TN, preferred_element_type=F32)
        out_t = acc / l
        out_t = jnp.concatenate(
            [out_t[hh * HEAD_DIM:(hh + 1) * HEAD_DIM, hh * blk:(hh + 1) * blk]
             for hh in range(HEADS_PER_LANE_GROUP)], axis=0)
        o_ref[0, t * blk:(t + 1) * blk, lanes(c)] = out_t.T.astype(BF16)

    def variant(i):
        chains = [(MOBA_Q_BLOCKS * i + t, t, c) for t in range(MOBA_Q_BLOCKS) for c in range(MOBA_LANE_GROUPS)]
        pending = [scores(*ch) for ch in chains[:MOBA_SCORE_LOOKAHEAD]]
        for idx, (qi, t, c) in enumerate(chains):
            if idx + MOBA_SCORE_LOOKAHEAD < len(chains):
                pending.append(scores(*chains[idx + MOBA_SCORE_LOOKAHEAD]))
            values(qi, t, c, *probs(qi, *pending[idx]))

    for i in range(n_blk // MOBA_Q_BLOCKS):
        pl.when(step == i)(functools.partial(variant, i))


def _moba(q, k, v, kmean, batch, seq):
    n_blk = seq // MOBA_BLOCK
    width = MOBA_LANE_GROUPS * V7X_LANES
    q3 = q.reshape(batch, seq, ATTN_WIDTH)
    k3 = k.reshape(batch, seq, ATTN_WIDTH)
    v3 = v.reshape(batch, seq, ATTN_WIDTH)
    km3 = kmean.reshape(batch, n_blk, ATTN_WIDTH)
    q_spec = pl.BlockSpec((1, MOBA_Q_BLOCKS * MOBA_BLOCK, width), lambda b, hg, i: (b, i, hg))
    kv_spec = pl.BlockSpec((1, seq, width), lambda b, hg, i: (b, 0, hg))
    out = pl.pallas_call(
        functools.partial(_moba_body, n_blk=n_blk),
        grid=(batch, ATTN_WIDTH // width, n_blk // MOBA_Q_BLOCKS),
        in_specs=[q_spec, kv_spec, kv_spec,
                  pl.BlockSpec((1, n_blk, width), lambda b, hg, i: (b, 0, hg))],
        out_specs=q_spec,
        out_shape=jax.ShapeDtypeStruct((batch, seq, ATTN_WIDTH), BF16),
        compiler_params=pltpu.CompilerParams(
            dimension_semantics=("parallel", "parallel", "arbitrary"),
            vmem_limit_bytes=V7X_VMEM_LIMIT_BYTES),
        name="moba",
    )(q3, k3, v3, km3)
    return out.reshape(batch * seq, ATTN_WIDTH)


def _rope_tables(seq):
    half = ROT_DIM // 2
    pos = jnp.arange(seq, dtype=F32)
    inv_freq = ROPE_THETA ** (-jnp.arange(0, ROT_DIM, 2, dtype=F32) / ROT_DIM)
    ang = pos[:, None] * inv_freq[None, :]
    cos, sin = jnp.cos(ang), jnp.sin(ang)
    ones = jnp.ones((seq, HEAD_DIM - ROT_DIM), F32)
    zeros = jnp.zeros((seq, HEAD_DIM - ROT_DIM), F32)
    zhalf = jnp.zeros((seq, half), F32)
    cos_h = jnp.concatenate([cos, cos, ones], axis=1)
    sup_h = jnp.concatenate([-sin, zhalf, zeros], axis=1)
    sdn_h = jnp.concatenate([zhalf, sin, zeros], axis=1)
    rep = V7X_LANES // HEAD_DIM
    return jnp.tile(cos_h, (1, rep)), jnp.tile(sup_h, (1, rep)), jnp.tile(sdn_h, (1, rep))


def _block_diag(w):
    n, h, d, _ = w.shape
    eye = jnp.eye(h, dtype=w.dtype)
    return jnp.einsum("nhij,hg->nhigj", w, eye).reshape(n, h * d, h * d)


def kernel(x, ffn1_w_gate, ffn1_w_up, ffn1_w_down, ffn2_w_gate, ffn2_w_up, ffn2_w_down, norm_pre, norm_post,
           ab_w_in, ab_w_out, lru_conv_w, lru_conv_b, lru_gate_a_w, lru_gate_a_b, lru_gate_x_w, lru_gate_x_b,
           lru_lambda, c_w_in, c_conv_w, c_w_out):
    batch, seq, d = x.shape
    depth = norm_pre.shape[0]
    assert d == D_MODEL and seq % ROW_TILE == 0 and ROW_TILE % MOBA_BLOCK == 0
    assert (batch * seq) % FFN_ROW_TILE == 0 and FFN_ROW_TILE % GROUP_ROWS == 0 and ROW_TILE % GROUP_ROWS == 0
    assert seq // MOBA_BLOCK > MOBA_TOPK and (seq // MOBA_BLOCK) % MOBA_Q_BLOCKS == 0

    ffn1 = (ffn1_w_gate, ffn1_w_up, ffn1_w_down)
    ffn2 = (ffn2_w_gate, ffn2_w_up, ffn2_w_down)
    ab_in, ab_out = ab_w_in.astype(BF16), ab_w_out.astype(BF16)
    c_in, c_out = c_w_in.astype(BF16), c_w_out.astype(BF16)
    wa = _block_diag(lru_gate_a_w).astype(BF16)
    wx = _block_diag(lru_gate_x_w).astype(BF16)
    n_even = ab_w_in.shape[0]
    ba = lru_gate_a_b.reshape(n_even, 1, LRU_WIDTH)
    bx = lru_gate_x_b.reshape(n_even, 1, LRU_WIDTH)
    lam = lru_lambda.reshape(n_even, 1, LRU_WIDTH)
    conv_b = lru_conv_b.reshape(n_even, 1, LRU_WIDTH)
    cos, sup, sdn = _rope_tables(seq)

    xs = x.reshape(batch * seq, d)
    for l in range(depth):
        pre = lambda j: norm_pre[l, j].reshape(1, d)
        post = lambda j: norm_post[l, j].reshape(1, d)
        xs = _ffn(xs, pre(0), post(0), *ffn1, l)
        mixer = None
        if l % 2 == 0:
            e = l // 2
            q, k, v, kmean, rec = _proj_lru(xs, pre(1), ab_in, cos, sup, sdn, lru_conv_w, conv_b,
                                            wa, wx, ba, bx, lam, e, batch, seq)
            attn = _moba(q, k, v, kmean, batch, seq)
            mixer = (attn, rec, post(1), ab_out, e)
        else:
            xs = _sconv(xs, pre(1), post(1), c_in, c_conv_w, c_out, l // 2, batch, seq)
        xs = _ffn(xs, pre(2), post(2), *ffn2, l, mixer=mixer)
    return xs.reshape(batch, seq, d)
```

```python
import functools

import jax
import jax.numpy as jnp
from jax import lax
from jax.experimental import pallas as pl
from jax.experimental.pallas import tpu as pltpu

F32 = jnp.float32
BF16 = jnp.bfloat16

NORM_EPS = 1e-6
D_MODEL = 1024
D_FF = 2816

ATTN_HEADS = 8
HEAD_DIM = 64
ATTN_WIDTH = ATTN_HEADS * HEAD_DIM
ROT_DIM = HEAD_DIM // 4
ROPE_THETA = 500000.0
MOBA_BLOCK = 256
MOBA_TOPK = 3
MASK_VALUE = -1e30

LRU_WIDTH = 512
LRU_HEADS = 8
LRU_CONV = 4
LRU_C = 8.0
AB_IN = 3 * ATTN_WIDTH + 2 * LRU_WIDTH

SC_CONV = 3

V7X_LANES = 128
V7X_SUBLANES = 8
V7X_MXU_DIM = 256
V7X_VMEM_LIMIT_BYTES = 56 * 1024 * 1024

ROW_TILE = 512
FFN_ROW_TILE = 1024
GROUP_ROWS = 256
FF_CHUNKS = (6 * V7X_MXU_DIM, 5 * V7X_MXU_DIM)
assert sum(FF_CHUNKS) == D_FF
HEADS_PER_LANE_GROUP = V7X_LANES // HEAD_DIM
MOBA_LANE_GROUPS = 4
MOBA_Q_BLOCKS = 4
MOBA_SCORE_LOOKAHEAD = 2
LOG2_E = 1.4426950408889634


def _rms(x, g):
    return x * lax.rsqrt(jnp.mean(x * x, axis=-1, keepdims=True) + NORM_EPS) * g


def _dot(a, b):
    return jnp.dot(a, b, preferred_element_type=F32)


def _resident(shape, index_map):
    return pl.BlockSpec(shape, index_map, pipeline_mode=pl.Buffered(1))


def _half_ffn_step(x, gpre_ref, gpost_ref, wg_ref, wu_ref, wd_ref):
    h = _rms(x, gpre_ref[...]).astype(BF16)
    acc = None
    for c in range(len(FF_CHUNKS)):
        sl = slice(sum(FF_CHUNKS[:c]), sum(FF_CHUNKS[:c + 1]))
        g = _dot(h, wg_ref[:, sl])
        u = _dot(h, wu_ref[:, sl])
        a = (g * jax.nn.sigmoid(g) * u).astype(BF16)
        d = _dot(a, wd_ref[sl, :])
        acc = d if acc is None else acc + d
    return x + 0.5 * _rms(acc, gpost_ref[...])


def _row_groups_of(ref):
    return [slice(r, r + GROUP_ROWS) for r in range(0, ref.shape[0], GROUP_ROWS)]


def _ffn_body(x_ref, gpre_ref, gpost_ref, wg_ref, wu_ref, wd_ref, o_ref):
    for rs in _row_groups_of(x_ref):
        o_ref[rs, :] = _half_ffn_step(x_ref[rs, :], gpre_ref, gpost_ref, wg_ref, wu_ref, wd_ref)


def _mix_ffn_body(x_ref, attn_ref, rec_ref, gmix_ref, wo_ref, gpre_ref, gpost_ref, wg_ref, wu_ref, wd_ref, o_ref):
    xs = []
    for rs in _row_groups_of(x_ref):
        mix = _dot(attn_ref[rs, :], wo_ref[:ATTN_WIDTH, :]) + _dot(rec_ref[rs, :], wo_ref[ATTN_WIDTH:, :])
        xs.append(x_ref[rs, :] + _rms(mix, gmix_ref[...]))
    for rs, x in zip(_row_groups_of(x_ref), xs):
        o_ref[rs, :] = _half_ffn_step(x, gpre_ref, gpost_ref, wg_ref, wu_ref, wd_ref)


def _ffn(x, gpre, gpost, wg, wu, wd, layer, mixer=None):
    m, d = x.shape
    tile = FFN_ROW_TILE
    row_spec = pl.BlockSpec((tile, d), lambda i: (i, 0))
    vec_spec = _resident((1, d), lambda i: (0, 0))
    ffn_specs = [
        vec_spec, vec_spec,
        _resident((None, d, D_FF), lambda i: (layer, 0, 0)),
        _resident((None, d, D_FF), lambda i: (layer, 0, 0)),
        _resident((None, D_FF, d), lambda i: (layer, 0, 0)),
    ]
    if mixer is None:
        body, name = _ffn_body, "ffn"
        in_specs = [row_spec] + ffn_specs
        args = (x, gpre, gpost, wg, wu, wd)
    else:
        attn, rec, gmix, w_out, mixer_layer = mixer
        body, name = _mix_ffn_body, "mix_ffn"
        half_spec = pl.BlockSpec((tile, ATTN_WIDTH), lambda i: (i, 0))
        in_specs = [row_spec, half_spec, half_spec, vec_spec,
                    _resident((None, ATTN_WIDTH + LRU_WIDTH, d), lambda i: (mixer_layer, 0, 0))] + ffn_specs
        args = (x, attn, rec, gmix, w_out, gpre, gpost, wg, wu, wd)
    return pl.pallas_call(
        body,
        grid=(m // tile,),
        in_specs=in_specs,
        out_specs=row_spec,
        out_shape=jax.ShapeDtypeStruct((m, d), F32),
        compiler_params=pltpu.CompilerParams(
            dimension_semantics=("parallel",), vmem_limit_bytes=V7X_VMEM_LIMIT_BYTES),
        name=name,
    )(*args)


def _sconv_body(x_ref, gpre_ref, gpost_ref, win_ref, cw_ref, wout_ref, o_ref, tail_ref):
    @pl.when(pl.program_id(1) == 0)
    def _():
        tail_ref[...] = jnp.zeros_like(tail_ref)

    d = x_ref.shape[-1]
    tail = tail_ref[...]
    groups = _row_groups_of(x_ref)
    gated = []
    for rs in groups:
        h = _rms(x_ref[rs, :], gpre_ref[...]).astype(BF16)
        p_conv = _dot(h, win_ref[:, d:])
        gate_b = _dot(h, win_ref[:, :d])
        cx = p_conv[:, :d] * p_conv[:, d:]
        cx3 = _row_groups(cx)
        conv = cw_ref[SC_CONV - 1:SC_CONV, :] * cx
        for k in range(1, SC_CONV):
            conv = conv + cw_ref[SC_CONV - 1 - k:SC_CONV - k, :] * _shift_rows_grouped(cx3, tail, k).reshape(cx.shape)
        tail = cx3[-1:]
        gated.append((gate_b * conv).astype(BF16))
    for rs, y in zip(groups, gated):
        o_ref[rs, :] = x_ref[rs, :] + _rms(_dot(y, wout_ref[...]), gpost_ref[...])
    tail_ref[...] = tail


def _sconv(x, gpre, gpost, w_in, conv_w, w_out, layer, batch, seq):
    m, d = x.shape
    n_s = seq // ROW_TILE
    row_spec = pl.BlockSpec((ROW_TILE, d), lambda b, s: (b * n_s + s, 0))
    vec_spec = _resident((1, d), lambda b, s: (0, 0))
    return pl.pallas_call(
        _sconv_body,
        grid=(batch, n_s),
        in_specs=[
            row_spec, vec_spec, vec_spec,
            _resident((None, d, 3 * d), lambda b, s: (layer, 0, 0)),
            _resident((None, SC_CONV, d), lambda b, s: (layer, 0, 0)),
            _resident((None, d, d), lambda b, s: (layer, 0, 0)),
        ],
        out_specs=row_spec,
        out_shape=jax.ShapeDtypeStruct((m, d), F32),
        scratch_shapes=[pltpu.VMEM((1, V7X_SUBLANES, d), F32)],
        compiler_params=pltpu.CompilerParams(
            dimension_semantics=("arbitrary", "arbitrary"), vmem_limit_bytes=V7X_VMEM_LIMIT_BYTES),
        name="sconv",
    )(x, gpre, gpost, w_in, conv_w, w_out)


def _rope(z, cos, sin_up, sin_dn):
    half = ROT_DIM // 2
    outs = []
    for c in range(ATTN_WIDTH // V7X_LANES):
        zc = z[:, c * V7X_LANES:(c + 1) * V7X_LANES]
        up = pltpu.roll(zc, V7X_LANES - half, axis=1)
        dn = pltpu.roll(zc, half, axis=1)
        outs.append(zc * cos + up * sin_up + dn * sin_dn)
    return jnp.concatenate(outs, axis=1)


def _row_groups(z):
    return z.reshape(z.shape[0] // V7X_SUBLANES, V7X_SUBLANES, z.shape[1])


def _shift_rows_grouped(cur, tail, k):
    sub = lax.broadcasted_iota(jnp.int32, (1, V7X_SUBLANES, 1), 1)
    rot = pltpu.roll(cur, k, axis=1)
    prev = jnp.concatenate([pltpu.roll(tail, k, axis=1), rot[:-1]], axis=0)
    return jnp.where(sub >= k, rot, prev)


def _scan_rows(a, u, h_in):
    sub = lax.broadcasted_iota(jnp.int32, (1, V7X_SUBLANES, 1), 1)
    d = 1
    while d < V7X_SUBLANES:
        keep = sub >= d
        a_s = jnp.where(keep, pltpu.roll(a, d, axis=1), 1.0)
        u_s = jnp.where(keep, pltpu.roll(u, d, axis=1), 0.0)
        u = a * u_s + u
        a = a * a_s
        d *= 2
    h = h_in
    out = []
    for g in range(a.shape[0]):
        hg = u[g] + a[g] * h
        out.append(hg)
        h = hg[V7X_SUBLANES - 1:, :]
    return jnp.concatenate(out, axis=0), h


def _proj_lru_body(x_ref, gpre_ref, win_ref, cos_ref, sup_ref, sdn_ref, cw_ref, cb_ref,
                   wa_ref, wx_ref, ba_ref, bx_ref, lam_ref,
                   q_ref, k_ref, v_ref, km_ref, rec_ref, xtail_ref, hprev_ref):
    @pl.when(pl.program_id(1) == 0)
    def _():
        xtail_ref[...] = jnp.zeros_like(xtail_ref)
        hprev_ref[...] = jnp.zeros_like(hprev_ref)

    w = ATTN_WIDTH
    xtail = xtail_ref[...]
    h_state = hprev_ref[...]
    softplus_neg_lam = jax.nn.softplus(-lam_ref[...])
    for g in range(ROW_TILE // MOBA_BLOCK):
        rs = slice(g * MOBA_BLOCK, (g + 1) * MOBA_BLOCK)
        h = _rms(x_ref[rs, :], gpre_ref[...]).astype(BF16)
        p_lru = _dot(h, win_ref[:, 3 * w:])
        p = _dot(h, win_ref[:, :3 * w])

        cos, sup, sdn = cos_ref[rs, :], sup_ref[rs, :], sdn_ref[rs, :]
        q_ref[rs, :] = (_rope(p[:, :w], cos, sup, sdn) * (HEAD_DIM ** -0.5 * LOG2_E)).astype(BF16)
        k = _rope(p[:, w:2 * w], cos, sup, sdn)
        k_ref[rs, :] = k.astype(BF16)
        v_ref[rs, :] = p[:, 2 * w:3 * w].astype(BF16)
        km_ref[g] = jnp.mean(k, axis=0, keepdims=True)

        lx = p_lru[:, :LRU_WIDTH]
        lg = p_lru[:, LRU_WIDTH:]
        lx3 = _row_groups(lx)
        xc = cw_ref[LRU_CONV - 1:LRU_CONV, :] * lx + cb_ref[...]
        for kk in range(1, LRU_CONV):
            shifted = _shift_rows_grouped(lx3, xtail, kk).reshape(lx.shape)
            xc = xc + cw_ref[LRU_CONV - 1 - kk:LRU_CONV - kk, :] * shifted
        xtail = lx3[-1:]

        xcb = xc.astype(BF16)
        r = jax.nn.sigmoid(_dot(xcb, wa_ref[...]) + ba_ref[...])
        i = jax.nn.sigmoid(_dot(xcb, wx_ref[...]) + bx_ref[...])
        a = jnp.exp((-LRU_C) * r * softplus_neg_lam)
        z = 1.0 - a * a
        root = jnp.where(z > 0.0, z * lax.rsqrt(z), 0.0)
        u = root * (i * xc)
        hs, h_state = _scan_rows(_row_groups(a), _row_groups(u), h_state)
        rec_ref[rs, :] = (hs * jax.nn.gelu(lg)).astype(BF16)
    xtail_ref[...] = xtail
    hprev_ref[...] = h_state


def _proj_lru(x, gpre, w_in, cos, sup, sdn, conv_w, conv_b, wa, wx, ba, bx, lam, layer, batch, seq):
    m, d = x.shape
    n_s = seq // ROW_TILE
    blocks_per_tile = ROW_TILE // MOBA_BLOCK

    def rows(width):
        return pl.BlockSpec((ROW_TILE, width), lambda b, s: (b * n_s + s, 0))

    def res(shape):
        zeros = (0,) * (len(shape) - 1)
        return _resident(shape, lambda b, s: (layer,) + zeros)

    tab_spec = pl.BlockSpec((ROW_TILE, V7X_LANES), lambda b, s: (s, 0))
    return pl.pallas_call(
        _proj_lru_body,
        grid=(batch, n_s),
        in_specs=[
            rows(d), _resident((1, d), lambda b, s: (0, 0)),
            res((None, d, AB_IN)),
            tab_spec, tab_spec, tab_spec,
            res((None, LRU_CONV, LRU_WIDTH)), res((None, 1, LRU_WIDTH)),
            res((None, LRU_WIDTH, LRU_WIDTH)), res((None, LRU_WIDTH, LRU_WIDTH)),
            res((None, 1, LRU_WIDTH)), res((None, 1, LRU_WIDTH)), res((None, 1, LRU_WIDTH)),
        ],
        out_specs=[
            rows(ATTN_WIDTH), rows(ATTN_WIDTH), rows(ATTN_WIDTH),
            pl.BlockSpec((blocks_per_tile, 1, ATTN_WIDTH), lambda b, s: (b * n_s + s, 0, 0)),
            rows(LRU_WIDTH),
        ],
        out_shape=[
            jax.ShapeDtypeStruct((m, ATTN_WIDTH), BF16),
            jax.ShapeDtypeStruct((m, ATTN_WIDTH), BF16),
            jax.ShapeDtypeStruct((m, ATTN_WIDTH), BF16),
            jax.ShapeDtypeStruct((m // MOBA_BLOCK, 1, ATTN_WIDTH), F32),
            jax.ShapeDtypeStruct((m, LRU_WIDTH), BF16),
        ],
        scratch_shapes=[pltpu.VMEM((1, V7X_SUBLANES, LRU_WIDTH), F32),
                        pltpu.VMEM((1, LRU_WIDTH), F32)],
        compiler_params=pltpu.CompilerParams(
            dimension_semantics=("arbitrary", "arbitrary"), vmem_limit_bytes=V7X_VMEM_LIMIT_BYTES),
        name="proj_lru",
    )(x, gpre, w_in, cos, sup, sdn, conv_w, conv_b, wa, wx, ba, bx, lam)


_NT = (((1,), (1,)), ((), ()))
_TN = (((0,), (0,)), ((), ()))


def _block_gate_bias(q2, km, qi, n_blk):
    nq = q2.shape[0]
    km_hi = km.astype(BF16).astype(F32)
    km_mid = (km - km_hi).astype(BF16).astype(F32)
    km_lo = (km - km_hi - km_mid).astype(BF16).astype(F32)
    km3 = jnp.concatenate([km_hi, km_mid, km_lo], axis=0).astype(BF16)
    g3 = lax.dot_general(km3, q2, _NT, preferred_element_type=F32)
    g = g3[:n_blk] + g3[n_blk:2 * n_blk] + g3[2 * n_blk:]
    bid = lax.broadcasted_iota(jnp.int32, (n_blk, nq), 0)
    g = jnp.where(bid < qi, g, MASK_VALUE)
    bias = jnp.full((n_blk, nq), MASK_VALUE, F32)
    for j in range(qi):
        gj = g[j:j + 1, :]
        beats = (g > gj) | ((g == gj) & (bid < j))
        rank = jnp.sum(beats.astype(jnp.int32), axis=0, keepdims=True)
        bias = jnp.where((bid == j) & (rank < MOBA_TOPK), 0.0, bias)
    return bias


def _moba_body(q_ref, k_ref, v_ref, km_ref, o_ref, *, n_blk):
    step = pl.program_id(2)
    blk = MOBA_BLOCK
    nq = HEADS_PER_LANE_GROUP * blk
    lane = lax.broadcasted_iota(jnp.int32, (1, V7X_LANES), 1)
    kpos = lax.broadcasted_iota(jnp.int32, (blk, nq), 0)
    qpos = lax.broadcasted_iota(jnp.int32, (blk, nq), 1) % blk
    causal = kpos <= qpos

    def lanes(c):
        return slice(c * V7X_LANES, (c + 1) * V7X_LANES)

    def scores(qi, t, c):
        q = q_ref[0, t * blk:(t + 1) * blk, lanes(c)]
        zero = jnp.zeros_like(q)
        q2 = jnp.concatenate(
            [jnp.where((lane // HEAD_DIM) == hh, q, zero) for hh in range(HEADS_PER_LANE_GROUP)], axis=0)
        bias = _block_gate_bias(q2, km_ref[0, :, lanes(c)], qi, n_blk)
        return lax.dot_general(k_ref[0, :(qi + 1) * blk, lanes(c)], q2, _NT, preferred_element_type=F32), bias

    def probs(qi, s, bias):
        own = jnp.where(causal, s[qi * blk:], MASK_VALUE)
        tops = [jnp.max(s[j * blk:(j + 1) * blk], axis=0, keepdims=True) + bias[j:j + 1, :] for j in range(qi)]
        tops.append(jnp.max(own, axis=0, keepdims=True))
        m = functools.reduce(jnp.maximum, tops)
        parts = [jnp.exp2(s[j * blk:(j + 1) * blk] - (m - bias[j:j + 1, :])) for j in range(qi)]
        parts.append(jnp.exp2(own - m))
        p = jnp.concatenate(parts, axis=0)
        return p.astype(BF16), jnp.sum(p, axis=0, keepdims=True)

    def values(qi, t, c, p, l):
        acc = lax.dot_general(v_ref[0, :(qi + 1) * blk, lanes(c)], p, _TN, preferred_element_type=F32)
        out_t = acc / l
        out_t = jnp.concatenate(
            [out_t[hh * HEAD_DIM:(hh + 1) * HEAD_DIM, hh * blk:(hh + 1) * blk]
             for hh in range(HEADS_PER_LANE_GROUP)], axis=0)
        o_ref[0, t * blk:(t + 1) * blk, lanes(c)] = out_t.T.astype(BF16)

    def variant(i):
        chains = [(MOBA_Q_BLOCKS * i + t, t, c) for t in range(MOBA_Q_BLOCKS) for c in range(MOBA_LANE_GROUPS)]
        pending = [scores(*ch) for ch in chains[:MOBA_SCORE_LOOKAHEAD]]
        for idx, (qi, t, c) in enumerate(chains):
            if idx + MOBA_SCORE_LOOKAHEAD < len(chains):
                pending.append(scores(*chains[idx + MOBA_SCORE_LOOKAHEAD]))
            values(qi, t, c, *probs(qi, *pending[idx]))

    for i in range(n_blk // MOBA_Q_BLOCKS):
        pl.when(step == i)(functools.partial(variant, i))


def _moba(q, k, v, kmean, batch, seq):
    n_blk = seq // MOBA_BLOCK
    width = MOBA_LANE_GROUPS * V7X_LANES
    q3 = q.reshape(batch, seq, ATTN_WIDTH)
    k3 = k.reshape(batch, seq, ATTN_WIDTH)
    v3 = v.reshape(batch, seq, ATTN_WIDTH)
    km3 = kmean.reshape(batch, n_blk, ATTN_WIDTH)
    q_spec = pl.BlockSpec((1, MOBA_Q_BLOCKS * MOBA_BLOCK, width), lambda b, hg, i: (b, i, hg))
    kv_spec = pl.BlockSpec((1, seq, width), lambda b, hg, i: (b, 0, hg))
    out = pl.pallas_call(
        functools.partial(_moba_body, n_blk=n_blk),
        grid=(batch, ATTN_WIDTH // width, n_blk // MOBA_Q_BLOCKS),
        in_specs=[q_spec, kv_spec, kv_spec,
                  pl.BlockSpec((1, n_blk, width), lambda b, hg, i: (b, 0, hg))],
        out_specs=q_spec,
        out_shape=jax.ShapeDtypeStruct((batch, seq, ATTN_WIDTH), BF16),
        compiler_params=pltpu.CompilerParams(
            dimension_semantics=("parallel", "parallel", "arbitrary"),
            vmem_limit_bytes=V7X_VMEM_LIMIT_BYTES),
        name="moba",
    )(q3, k3, v3, km3)
    return out.reshape(batch * seq, ATTN_WIDTH)


def _rope_tables(seq):
    half = ROT_DIM // 2
    pos = jnp.arange(seq, dtype=F32)
    inv_freq = ROPE_THETA ** (-jnp.arange(0, ROT_DIM, 2, dtype=F32) / ROT_DIM)
    ang = pos[:, None] * inv_freq[None, :]
    cos, sin = jnp.cos(ang), jnp.sin(ang)
    ones = jnp.ones((seq, HEAD_DIM - ROT_DIM), F32)
    zeros = jnp.zeros((seq, HEAD_DIM - ROT_DIM), F32)
    zhalf = jnp.zeros((seq, half), F32)
    cos_h = jnp.concatenate([cos, cos, ones], axis=1)
    sup_h = jnp.concatenate([-sin, zhalf, zeros], axis=1)
    sdn_h = jnp.concatenate([zhalf, sin, zeros], axis=1)
    rep = V7X_LANES // HEAD_DIM
    return jnp.tile(cos_h, (1, rep)), jnp.tile(sup_h, (1, rep)), jnp.tile(sdn_h, (1, rep))


def _block_diag(w):
    n, h, d, _ = w.shape
    eye = jnp.eye(h, dtype=w.dtype)
    return jnp.einsum("nhij,hg->nhigj", w, eye).reshape(n, h * d, h * d)


def kernel(x, ffn1_w_gate, ffn1_w_up, ffn1_w_down, ffn2_w_gate, ffn2_w_up, ffn2_w_down, norm_pre, norm_post,
           ab_w_in, ab_w_out, lru_conv_w, lru_conv_b, lru_gate_a_w, lru_gate_a_b, lru_gate_x_w, lru_gate_x_b,
           lru_lambda, c_w_in, c_conv_w, c_w_out):
    batch, seq, d = x.shape
    depth = norm_pre.shape[0]
    assert d == D_MODEL and seq % ROW_TILE == 0 and ROW_TILE % MOBA_BLOCK == 0
    assert (batch * seq) % FFN_ROW_TILE == 0 and FFN_ROW_TILE % GROUP_ROWS == 0 and ROW_TILE % GROUP_ROWS == 0
    assert seq // MOBA_BLOCK > MOBA_TOPK and (seq // MOBA_BLOCK) % MOBA_Q_BLOCKS == 0

    ffn1 = [w.astype(BF16) for w in (ffn1_w_gate, ffn1_w_up, ffn1_w_down)]
    ffn2 = [w.astype(BF16) for w in (ffn2_w_gate, ffn2_w_up, ffn2_w_down)]
    ab_in, ab_out = ab_w_in.astype(BF16), ab_w_out.astype(BF16)
    c_in, c_out = c_w_in.astype(BF16), c_w_out.astype(BF16)
    wa = _block_diag(lru_gate_a_w).astype(BF16)
    wx = _block_diag(lru_gate_x_w).astype(BF16)
    n_even = ab_w_in.shape[0]
    ba = lru_gate_a_b.reshape(n_even, 1, LRU_WIDTH)
    bx = lru_gate_x_b.reshape(n_even, 1, LRU_WIDTH)
    lam = lru_lambda.reshape(n_even, 1, LRU_WIDTH)
    conv_b = lru_conv_b.reshape(n_even, 1, LRU_WIDTH)
    cos, sup, sdn = _rope_tables(seq)

    xs = x.reshape(batch * seq, d)
    for l in range(depth):
        pre = lambda j: norm_pre[l, j].reshape(1, d)
        post = lambda j: norm_post[l, j].reshape(1, d)
        xs = _ffn(xs, pre(0), post(0), *ffn1, l)
        mixer = None
        if l % 2 == 0:
            e = l // 2
            q, k, v, kmean, rec = _proj_lru(xs, pre(1), ab_in, cos, sup, sdn, lru_conv_w, conv_b,
                                            wa, wx, ba, bx, lam, e, batch, seq)
            attn = _moba(q, k, v, kmean, batch, seq)
            mixer = (attn, rec, post(1), ab_out, e)
        else:
            xs = _sconv(xs, pre(1), post(1), c_in, c_conv_w, c_out, l // 2, batch, seq)
        xs = _ffn(xs, pre(2), post(2), *ffn2, l, mixer=mixer)
    return xs.reshape(batch, seq, d)
```

```python
import functools

import jax
import jax.numpy as jnp
from jax import lax
from jax.experimental import pallas as pl
from jax.experimental.pallas import tpu as pltpu

F32 = jnp.float32
BF16 = jnp.bfloat16

NORM_EPS = 1e-6
D_MODEL = 1024
D_FF = 2816

ATTN_HEADS = 8
HEAD_DIM = 64
ATTN_WIDTH = ATTN_HEADS * HEAD_DIM
ROT_DIM = HEAD_DIM // 4
ROPE_THETA = 500000.0
MOBA_BLOCK = 256
MOBA_TOPK = 3
MASK_VALUE = -1e30

LRU_WIDTH = 512
LRU_HEADS = 8
LRU_CONV = 4
LRU_C = 8.0
AB_IN = 3 * ATTN_WIDTH + 2 * LRU_WIDTH

SC_CONV = 3

V7X_LANES = 128
V7X_SUBLANES = 8
V7X_MXU_DIM = 256
V7X_VMEM_LIMIT_BYTES = 56 * 1024 * 1024

ROW_TILE = 1024
FFN_ROW_TILE = 1024
GROUP_ROWS = 256
FF_CHUNKS = (6 * V7X_MXU_DIM, 5 * V7X_MXU_DIM)
assert sum(FF_CHUNKS) == D_FF
HEADS_PER_LANE_GROUP = V7X_LANES // HEAD_DIM
MOBA_LANE_GROUPS = 4
MOBA_Q_BLOCKS = 4
MOBA_SCORE_LOOKAHEAD = 2
LOG2_E = 1.4426950408889634


def _rms(x, g):
    return x * lax.rsqrt(jnp.mean(x * x, axis=-1, keepdims=True) + NORM_EPS) * g


def _dot(a, b):
    return jnp.dot(a, b, preferred_element_type=F32)


def _resident(shape, index_map):
    return pl.BlockSpec(shape, index_map, pipeline_mode=pl.Buffered(1))


def _half_ffn_step(x, gpre_ref, gpost_ref, wg_ref, wu_ref, wd_ref):
    h = _rms(x, gpre_ref[...]).astype(BF16)
    acc = None
    for c in range(len(FF_CHUNKS)):
        sl = slice(sum(FF_CHUNKS[:c]), sum(FF_CHUNKS[:c + 1]))
        g = _dot(h, wg_ref[:, sl])
        u = _dot(h, wu_ref[:, sl])
        a = (g * jax.nn.sigmoid(g) * u).astype(BF16)
        d = _dot(a, wd_ref[sl, :])
        acc = d if acc is None else acc + d
    return x + 0.5 * _rms(acc, gpost_ref[...])


def _row_groups_of(ref):
    return [slice(r, r + GROUP_ROWS) for r in range(0, ref.shape[0], GROUP_ROWS)]


def _ffn_body(x_ref, gpre_ref, gpost_ref, wg_ref, wu_ref, wd_ref, o_ref):
    for rs in _row_groups_of(x_ref):
        o_ref[rs, :] = _half_ffn_step(x_ref[rs, :], gpre_ref, gpost_ref, wg_ref, wu_ref, wd_ref)


def _mix_ffn_body(x_ref, attn_ref, rec_ref, gmix_ref, wo_ref, gpre_ref, gpost_ref, wg_ref, wu_ref, wd_ref, o_ref):
    xs = []
    for rs in _row_groups_of(x_ref):
        mix = _dot(attn_ref[rs, :], wo_ref[:ATTN_WIDTH, :]) + _dot(rec_ref[rs, :], wo_ref[ATTN_WIDTH:, :])
        xs.append(x_ref[rs, :] + _rms(mix, gmix_ref[...]))
    for rs, x in zip(_row_groups_of(x_ref), xs):
        o_ref[rs, :] = _half_ffn_step(x, gpre_ref, gpost_ref, wg_ref, wu_ref, wd_ref)


def _ffn(x, gpre, gpost, wg, wu, wd, layer, mixer=None):
    m, d = x.shape
    tile = FFN_ROW_TILE
    row_spec = pl.BlockSpec((tile, d), lambda i: (i, 0))
    vec_spec = _resident((1, d), lambda i: (0, 0))
    ffn_specs = [
        vec_spec, vec_spec,
        _resident((None, d, D_FF), lambda i: (layer, 0, 0)),
        _resident((None, d, D_FF), lambda i: (layer, 0, 0)),
        _resident((None, D_FF, d), lambda i: (layer, 0, 0)),
    ]
    if mixer is None:
        body, name = _ffn_body, "ffn"
        in_specs = [row_spec] + ffn_specs
        args = (x, gpre, gpost, wg, wu, wd)
    else:
        attn, rec, gmix, w_out, mixer_layer = mixer
        body, name = _mix_ffn_body, "mix_ffn"
        half_spec = pl.BlockSpec((tile, ATTN_WIDTH), lambda i: (i, 0))
        in_specs = [row_spec, half_spec, half_spec, vec_spec,
                    _resident((None, ATTN_WIDTH + LRU_WIDTH, d), lambda i: (mixer_layer, 0, 0))] + ffn_specs
        args = (x, attn, rec, gmix, w_out, gpre, gpost, wg, wu, wd)
    return pl.pallas_call(
        body,
        grid=(m // tile,),
        in_specs=in_specs,
        out_specs=row_spec,
        out_shape=jax.ShapeDtypeStruct((m, d), F32),
        compiler_params=pltpu.CompilerParams(
            dimension_semantics=("parallel",), vmem_limit_bytes=V7X_VMEM_LIMIT_BYTES),
        name=name,
    )(*args)


def _sconv_body(x_ref, gpre_ref, gpost_ref, win_ref, cw_ref, wout_ref, o_ref, tail_ref):
    @pl.when(pl.program_id(1) == 0)
    def _():
        tail_ref[...] = jnp.zeros_like(tail_ref)

    d = x_ref.shape[-1]
    tail = tail_ref[...]
    groups = _row_groups_of(x_ref)
    gated = []
    for rs in groups:
        h = _rms(x_ref[rs, :], gpre_ref[...]).astype(BF16)
        p_conv = _dot(h, win_ref[:, d:])
        gate_b = _dot(h, win_ref[:, :d])
        cx = p_conv[:, :d] * p_conv[:, d:]
        cx3 = _row_groups(cx)
        conv = cw_ref[SC_CONV - 1:SC_CONV, :] * cx
        for k in range(1, SC_CONV):
            conv = conv + cw_ref[SC_CONV - 1 - k:SC_CONV - k, :] * _shift_rows_grouped(cx3, tail, k).reshape(cx.shape)
        tail = cx3[-1:]
        gated.append((gate_b * conv).astype(BF16))
    for rs, y in zip(groups, gated):
        o_ref[rs, :] = x_ref[rs, :] + _rms(_dot(y, wout_ref[...]), gpost_ref[...])
    tail_ref[...] = tail


def _sconv(x, gpre, gpost, w_in, conv_w, w_out, layer, batch, seq):
    m, d = x.shape
    n_s = seq // ROW_TILE
    row_spec = pl.BlockSpec((ROW_TILE, d), lambda b, s: (b * n_s + s, 0))
    vec_spec = _resident((1, d), lambda b, s: (0, 0))
    return pl.pallas_call(
        _sconv_body,
        grid=(batch, n_s),
        in_specs=[
            row_spec, vec_spec, vec_spec,
            _resident((None, d, 3 * d), lambda b, s: (layer, 0, 0)),
            _resident((None, SC_CONV, d), lambda b, s: (layer, 0, 0)),
            _resident((None, d, d), lambda b, s: (layer, 0, 0)),
        ],
        out_specs=row_spec,
        out_shape=jax.ShapeDtypeStruct((m, d), F32),
        scratch_shapes=[pltpu.VMEM((1, V7X_SUBLANES, d), F32)],
        compiler_params=pltpu.CompilerParams(
            dimension_semantics=("arbitrary", "arbitrary"), vmem_limit_bytes=V7X_VMEM_LIMIT_BYTES),
        name="sconv",
    )(x, gpre, gpost, w_in, conv_w, w_out)


def _rope(z, cos, sin_up, sin_dn):
    half = ROT_DIM // 2
    outs = []
    for c in range(ATTN_WIDTH // V7X_LANES):
        zc = z[:, c * V7X_LANES:(c + 1) * V7X_LANES]
        up = pltpu.roll(zc, V7X_LANES - half, axis=1)
        dn = pltpu.roll(zc, half, axis=1)
        outs.append(zc * cos + up * sin_up + dn * sin_dn)
    return jnp.concatenate(outs, axis=1)


def _row_groups(z):
    return z.reshape(z.shape[0] // V7X_SUBLANES, V7X_SUBLANES, z.shape[1])


def _shift_rows_grouped(cur, tail, k):
    sub = lax.broadcasted_iota(jnp.int32, (1, V7X_SUBLANES, 1), 1)
    rot = pltpu.roll(cur, k, axis=1)
    prev = jnp.concatenate([pltpu.roll(tail, k, axis=1), rot[:-1]], axis=0)
    return jnp.where(sub >= k, rot, prev)


def _scan_rows(a, u, h_in):
    sub = lax.broadcasted_iota(jnp.int32, (1, V7X_SUBLANES, 1), 1)
    d = 1
    while d < V7X_SUBLANES:
        keep = sub >= d
        a_s = jnp.where(keep, pltpu.roll(a, d, axis=1), 1.0)
        u_s = jnp.where(keep, pltpu.roll(u, d, axis=1), 0.0)
        u = a * u_s + u
        a = a * a_s
        d *= 2
    h = h_in
    out = []
    for g in range(a.shape[0]):
        hg = u[g] + a[g] * h
        out.append(hg)
        h = hg[V7X_SUBLANES - 1:, :]
    return jnp.concatenate(out, axis=0), h


def _proj_lru_body(x_ref, gpre_ref, win_ref, cos_ref, sup_ref, sdn_ref, cw_ref, cb_ref,
                   wa_ref, wx_ref, ba_ref, bx_ref, lam_ref,
                   q_ref, k_ref, v_ref, km_ref, rec_ref, xtail_ref, hprev_ref):
    @pl.when(pl.program_id(1) == 0)
    def _():
        xtail_ref[...] = jnp.zeros_like(xtail_ref)
        hprev_ref[...] = jnp.zeros_like(hprev_ref)

    w = ATTN_WIDTH
    xtail = xtail_ref[...]
    h_state = hprev_ref[...]
    softplus_neg_lam = jax.nn.softplus(-lam_ref[...])
    for g in range(ROW_TILE // MOBA_BLOCK):
        rs = slice(g * MOBA_BLOCK, (g + 1) * MOBA_BLOCK)
        h = _rms(x_ref[rs, :], gpre_ref[...]).astype(BF16)
        p_lru = _dot(h, win_ref[:, 3 * w:])
        p = _dot(h, win_ref[:, :3 * w])

        cos, sup, sdn = cos_ref[rs, :], sup_ref[rs, :], sdn_ref[rs, :]
        q_ref[rs, :] = (_rope(p[:, :w], cos, sup, sdn) * (HEAD_DIM ** -0.5 * LOG2_E)).astype(BF16)
        k = _rope(p[:, w:2 * w], cos, sup, sdn)
        k_ref[rs, :] = k.astype(BF16)
        v_ref[rs, :] = p[:, 2 * w:3 * w].astype(BF16)
        km_ref[g] = jnp.mean(k, axis=0, keepdims=True)

        lx = p_lru[:, :LRU_WIDTH]
        lg = p_lru[:, LRU_WIDTH:]
        lx3 = _row_groups(lx)
        xc = cw_ref[LRU_CONV - 1:LRU_CONV, :] * lx + cb_ref[...]
        for kk in range(1, LRU_CONV):
            shifted = _shift_rows_grouped(lx3, xtail, kk).reshape(lx.shape)
            xc = xc + cw_ref[LRU_CONV - 1 - kk:LRU_CONV - kk, :] * shifted
        xtail = lx3[-1:]

        xcb = xc.astype(BF16)
        r = jax.nn.sigmoid(_dot(xcb, wa_ref[...]) + ba_ref[...])
        i = jax.nn.sigmoid(_dot(xcb, wx_ref[...]) + bx_ref[...])
        a = jnp.exp((-LRU_C) * r * softplus_neg_lam)
        z = 1.0 - a * a
        root = jnp.where(z > 0.0, z * lax.rsqrt(z), 0.0)
        u = root * (i * xc)
        hs, h_state = _scan_rows(_row_groups(a), _row_groups(u), h_state)
        rec_ref[rs, :] = (hs * jax.nn.gelu(lg)).astype(BF16)
    xtail_ref[...] = xtail
    hprev_ref[...] = h_state


def _proj_lru(x, gpre, w_in, cos, sup, sdn, conv_w, conv_b, wa, wx, ba, bx, lam, layer, batch, seq):
    m, d = x.shape
    n_s = seq // ROW_TILE
    blocks_per_tile = ROW_TILE // MOBA_BLOCK

    def rows(width):
        return pl.BlockSpec((ROW_TILE, width), lambda b, s: (b * n_s + s, 0))

    def res(shape):
        zeros = (0,) * (len(shape) - 1)
        return _resident(shape, lambda b, s: (layer,) + zeros)

    tab_spec = pl.BlockSpec((ROW_TILE, V7X_LANES), lambda b, s: (s, 0))
    return pl.pallas_call(
        _proj_lru_body,
        grid=(batch, n_s),
        in_specs=[
            rows(d), _resident((1, d), lambda b, s: (0, 0)),
            res((None, d, AB_IN)),
            tab_spec, tab_spec, tab_spec,
            res((None, LRU_CONV, LRU_WIDTH)), res((None, 1, LRU_WIDTH)),
            res((None, LRU_WIDTH, LRU_WIDTH)), res((None, LRU_WIDTH, LRU_WIDTH)),
            res((None, 1, LRU_WIDTH)), res((None, 1, LRU_WIDTH)), res((None, 1, LRU_WIDTH)),
        ],
        out_specs=[
            rows(ATTN_WIDTH), rows(ATTN_WIDTH), rows(ATTN_WIDTH),
            pl.BlockSpec((blocks_per_tile, 1, ATTN_WIDTH), lambda b, s: (b * n_s + s, 0, 0)),
            rows(LRU_WIDTH),
        ],
        out_shape=[
            jax.ShapeDtypeStruct((m, ATTN_WIDTH), BF16),
            jax.ShapeDtypeStruct((m, ATTN_WIDTH), BF16),
            jax.ShapeDtypeStruct((m, ATTN_WIDTH), BF16),
            jax.ShapeDtypeStruct((m // MOBA_BLOCK, 1, ATTN_WIDTH), F32),
            jax.ShapeDtypeStruct((m, LRU_WIDTH), BF16),
        ],
        scratch_shapes=[pltpu.VMEM((1, V7X_SUBLANES, LRU_WIDTH), F32),
                        pltpu.VMEM((1, LRU_WIDTH), F32)],
        compiler_params=pltpu.CompilerParams(
            dimension_semantics=("arbitrary", "arbitrary"), vmem_limit_bytes=V7X_VMEM_LIMIT_BYTES),
        name="proj_lru",
    )(x, gpre, w_in, cos, sup, sdn, conv_w, conv_b, wa, wx, ba, bx, lam)


_NT = (((1,), (1,)), ((), ()))
_TN = (((0,), (0,)), ((), ()))


def _block_gate_bias(q2, km, qi, n_blk):
    nq = q2.shape[0]
    km_hi = km.astype(BF16).astype(F32)
    km_mid = (km - km_hi).astype(BF16).astype(F32)
    km_lo = (km - km_hi - km_mid).astype(BF16).astype(F32)
    km3 = jnp.concatenate([km_hi, km_mid, km_lo], axis=0).astype(BF16)
    g3 = lax.dot_general(km3, q2, _NT, preferred_element_type=F32)
    g = g3[:n_blk] + g3[n_blk:2 * n_blk] + g3[2 * n_blk:]
    bid = lax.broadcasted_iota(jnp.int32, (n_blk, nq), 0)
    g = jnp.where(bid < qi, g, MASK_VALUE)
    bias = jnp.full((n_blk, nq), MASK_VALUE, F32)
    for j in range(qi):
        gj = g[j:j + 1, :]
        beats = (g > gj) | ((g == gj) & (bid < j))
        rank = jnp.sum(beats.astype(jnp.int32), axis=0, keepdims=True)
        bias = jnp.where((bid == j) & (rank < MOBA_TOPK), 0.0, bias)
    return bias


def _moba_body(q_ref, k_ref, v_ref, km_ref, o_ref, *, n_blk):
    step = pl.program_id(2)
    blk = MOBA_BLOCK
    nq = HEADS_PER_LANE_GROUP * blk
    lane = lax.broadcasted_iota(jnp.int32, (1, V7X_LANES), 1)
    kpos = lax.broadcasted_iota(jnp.int32, (blk, nq), 0)
    qpos = lax.broadcasted_iota(jnp.int32, (blk, nq), 1) % blk
    causal = kpos <= qpos

    def lanes(c):
        return slice(c * V7X_LANES, (c + 1) * V7X_LANES)

    def scores(qi, t, c):
        q = q_ref[0, t * blk:(t + 1) * blk, lanes(c)]
        zero = jnp.zeros_like(q)
        q2 = jnp.concatenate(
            [jnp.where((lane // HEAD_DIM) == hh, q, zero) for hh in range(HEADS_PER_LANE_GROUP)], axis=0)
        bias = _block_gate_bias(q2, km_ref[0, :, lanes(c)], qi, n_blk)
        return lax.dot_general(k_ref[0, :(qi + 1) * blk, lanes(c)], q2, _NT, preferred_element_type=F32), bias

    def probs(qi, s, bias):
        own = jnp.where(causal, s[qi * blk:], MASK_VALUE)
        tops = [jnp.max(s[j * blk:(j + 1) * blk], axis=0, keepdims=True) + bias[j:j + 1, :] for j in range(qi)]
        tops.append(jnp.max(own, axis=0, keepdims=True))
        m = functools.reduce(jnp.maximum, tops)
        parts = [jnp.exp2(s[j * blk:(j + 1) * blk] - (m - bias[j:j + 1, :])) for j in range(qi)]
        parts.append(jnp.exp2(own - m))
        p = jnp.concatenate(parts, axis=0)
        return p.astype(BF16), jnp.sum(p, axis=0, keepdims=True)

    def values(qi, t, c, p, l):
        acc = lax.dot_general(v_ref[0, :(qi + 1) * blk, lanes(c)], p, _TN, preferred_element_type=F32)
        out_t = acc / l
        out_t = jnp.concatenate(
            [out_t[hh * HEAD_DIM:(hh + 1) * HEAD_DIM, hh * blk:(hh + 1) * blk]
             for hh in range(HEADS_PER_LANE_GROUP)], axis=0)
        o_ref[0, t * blk:(t + 1) * blk, lanes(c)] = out_t.T.astype(BF16)

    def variant(i):
        chains = [(MOBA_Q_BLOCKS * i + t, t, c) for t in range(MOBA_Q_BLOCKS) for c in range(MOBA_LANE_GROUPS)]
        pending = [scores(*ch) for ch in chains[:MOBA_SCORE_LOOKAHEAD]]
        for idx, (qi, t, c) in enumerate(chains):
            if idx + MOBA_SCORE_LOOKAHEAD < len(chains):
                pending.append(scores(*chains[idx + MOBA_SCORE_LOOKAHEAD]))
            values(qi, t, c, *probs(qi, *pending[idx]))

    for i in range(n_blk // MOBA_Q_BLOCKS):
        pl.when(step == i)(functools.partial(variant, i))


def _moba(q, k, v, kmean, batch, seq):
    n_blk = seq // MOBA_BLOCK
    width = MOBA_LANE_GROUPS * V7X_LANES
    q3 = q.reshape(batch, seq, ATTN_WIDTH)
    k3 = k.reshape(batch, seq, ATTN_WIDTH)
    v3 = v.reshape(batch, seq, ATTN_WIDTH)
    km3 = kmean.reshape(batch, n_blk, ATTN_WIDTH)
    q_spec = pl.BlockSpec((1, MOBA_Q_BLOCKS * MOBA_BLOCK, width), lambda b, hg, i: (b, i, hg))
    kv_spec = pl.BlockSpec((1, seq, width), lambda b, hg, i: (b, 0, hg))
    out = pl.pallas_call(
        functools.partial(_moba_body, n_blk=n_blk),
        grid=(batch, ATTN_WIDTH // width, n_blk // MOBA_Q_BLOCKS),
        in_specs=[q_spec, kv_spec, kv_spec,
                  pl.BlockSpec((1, n_blk, width), lambda b, hg, i: (b, 0, hg))],
        out_specs=q_spec,
        out_shape=jax.ShapeDtypeStruct((batch, seq, ATTN_WIDTH), BF16),
        compiler_params=pltpu.CompilerParams(
            dimension_semantics=("parallel", "parallel", "arbitrary"),
            vmem_limit_bytes=V7X_VMEM_LIMIT_BYTES),
        name="moba",
    )(q3, k3, v3, km3)
    return out.reshape(batch * seq, ATTN_WIDTH)


def _rope_tables(seq):
    half = ROT_DIM // 2
    pos = jnp.arange(seq, dtype=F32)
    inv_freq = ROPE_THETA ** (-jnp.arange(0, ROT_DIM, 2, dtype=F32) / ROT_DIM)
    ang = pos[:, None] * inv_freq[None, :]
    cos, sin = jnp.cos(ang), jnp.sin(ang)
    ones = jnp.ones((seq, HEAD_DIM - ROT_DIM), F32)
    zeros = jnp.zeros((seq, HEAD_DIM - ROT_DIM), F32)
    zhalf = jnp.zeros((seq, half), F32)
    cos_h = jnp.concatenate([cos, cos, ones], axis=1)
    sup_h = jnp.concatenate([-sin, zhalf, zeros], axis=1)
    sdn_h = jnp.concatenate([zhalf, sin, zeros], axis=1)
    rep = V7X_LANES // HEAD_DIM
    return jnp.tile(cos_h, (1, rep)), jnp.tile(sup_h, (1, rep)), jnp.tile(sdn_h, (1, rep))


def _block_diag(w):
    n, h, d, _ = w.shape
    eye = jnp.eye(h, dtype=w.dtype)
    return jnp.einsum("nhij,hg->nhigj", w, eye).reshape(n, h * d, h * d)


def kernel(x, ffn1_w_gate, ffn1_w_up, ffn1_w_down, ffn2_w_gate, ffn2_w_up, ffn2_w_down, norm_pre, norm_post,
           ab_w_in, ab_w_out, lru_conv_w, lru_conv_b, lru_gate_a_w, lru_gate_a_b, lru_gate_x_w, lru_gate_x_b,
           lru_lambda, c_w_in, c_conv_w, c_w_out):
    batch, seq, d = x.shape
    depth = norm_pre.shape[0]
    assert d == D_MODEL and seq % ROW_TILE == 0 and ROW_TILE % MOBA_BLOCK == 0
    assert (batch * seq) % FFN_ROW_TILE == 0 and FFN_ROW_TILE % GROUP_ROWS == 0 and ROW_TILE % GROUP_ROWS == 0
    assert seq // MOBA_BLOCK > MOBA_TOPK and (seq // MOBA_BLOCK) % MOBA_Q_BLOCKS == 0

    ffn1 = [w.astype(BF16) for w in (ffn1_w_gate, ffn1_w_up, ffn1_w_down)]
    ffn2 = [w.astype(BF16) for w in (ffn2_w_gate, ffn2_w_up, ffn2_w_down)]
    ab_in, ab_out = ab_w_in.astype(BF16), ab_w_out.astype(BF16)
    c_in, c_out = c_w_in.astype(BF16), c_w_out.astype(BF16)
    wa = _block_diag(lru_gate_a_w).astype(BF16)
    wx = _block_diag(lru_gate_x_w).astype(BF16)
    n_even = ab_w_in.shape[0]
    ba = lru_gate_a_b.reshape(n_even, 1, LRU_WIDTH)
    bx = lru_gate_x_b.reshape(n_even, 1, LRU_WIDTH)
    lam = lru_lambda.reshape(n_even, 1, LRU_WIDTH)
    conv_b = lru_conv_b.reshape(n_even, 1, LRU_WIDTH)
    cos, sup, sdn = _rope_tables(seq)

    xs = x.reshape(batch * seq, d)
    for l in range(depth):
        pre = lambda j: norm_pre[l, j].reshape(1, d)
        post = lambda j: norm_post[l, j].reshape(1, d)
        xs = _ffn(xs, pre(0), post(0), *ffn1, l)
        mixer = None
        if l % 2 == 0:
            e = l // 2
            q, k, v, kmean, rec = _proj_lru(xs, pre(1), ab_in, cos, sup, sdn, lru_conv_w, conv_b,
                                            wa, wx, ba, bx, lam, e, batch, seq)
            attn = _moba(q, k, v, kmean, batch, seq)
            mixer = (attn, rec, post(1), ab_out, e)
        else:
            xs = _sconv(xs, pre(1), post(1), c_in, c_conv_w, c_out, l // 2, batch, seq)
        xs = _ffn(xs, pre(2), post(2), *ffn2, l, mixer=mixer)
    return xs.reshape(batch, seq, d)
```
